```python
import math
import jax
import jax.numpy as jnp
from jax import lax
import numpy as np

D_MODEL = 1024
BATCH = 2
SEQ = 8192
DEPTH = 4
DEC_BATCH = 128
DEC_SEQ = 8
PAST_LEN = 2048
PAGE_SIZE = 128

HEAD_DIM = 64
MIX_W = D_MODEL
W_A = MIX_W // 4
W_B = MIX_W // 4
W_C = MIX_W // 4
W_D = MIX_W - W_A - W_B - W_C
H_A = W_A // HEAD_DIM
CHUNK = 128
POOL_WINDOWS = (2, 4, 8, 16)
N_POOL_GROUPS = len(POOL_WINDOWS)
G_B = W_B // N_POOL_GROUPS
POOL_BUF = max(POOL_WINDOWS) - 1
H_C = W_C // HEAD_DIM
DK_C = HEAD_DIM // 2
DV_C = HEAD_DIM
Q_C = H_C * 2 * DK_C
Q_BLOCK = 128
H_D = W_D // HEAD_DIM
N_D = HEAD_DIM
LORA_W = 64
LORA_A = 64
LORA_G = 128
N_COLS_D = 3 * W_D + LORA_W + LORA_A + LORA_G
OFF_A = 0
OFF_B = OFF_A + 2 * W_A
OFF_C = OFF_B + W_B
OFF_D = OFF_C + 2 * Q_C + W_C
N_IN = OFF_D + N_COLS_D
D_FF = 2816
N_EXPERTS = 8
TOP_K = 2
D_FF_EXPERT = 2816
MOE_BLOCK = 128
PLE_DIM = 256
NORM_EPS = 1e-6
RWKV_GN_EPS = 64e-5

kernel_name = 'hybrid_heads_decoder_step'


def rmsnorm(x, g, eps=NORM_EPS):
    xf = x.astype(jnp.float32)
    y = xf * lax.rsqrt(jnp.mean(xf * xf, axis=-1, keepdims=True) + eps) * g.astype(jnp.float32)
    return y.astype(x.dtype)


def swiglu(x, w1, w3, w2):
    return (jax.nn.silu(x @ w1) * (x @ w3)) @ w2


def chunk_mlp(u, v, vnorm_g, ws, bs):
    b, L, _ = u.shape
    u = jax.nn.gelu(u).reshape(b, L, H_A, HEAD_DIM)
    v = rmsnorm(jax.nn.gelu(v).reshape(b, L, H_A, HEAD_DIM), vnorm_g.reshape(H_A, HEAD_DIM))
    nc = -(-L // CHUNK)
    lp = nc * CHUNK
    vp = jnp.pad(v, ((0, 0), (0, lp - L), (0, 0), (0, 0))).reshape(b, nc, CHUNK, H_A, HEAD_DIM)
    causal = jnp.tril(jnp.ones((CHUNK, CHUNK), bool))
    wsm = jnp.where(causal, ws, 0.0)
    s = jnp.einsum('hts,bnshd->bnthd', wsm, vp) + bs.T[None, None, :, :, None]
    s = s.reshape(b, lp, H_A, HEAD_DIM)[:, :L]
    return (u * s).reshape(b, L, W_A), v.reshape(b, L, W_A)


def pool_mixer(z, buf, start_pos, proj, scale):
    b, L, _ = z.shape
    zc = jnp.concatenate([buf, z], axis=1)
    zf = zc.astype(jnp.float32)
    cs = jnp.concatenate([jnp.zeros((b, 1, W_B), jnp.float32), jnp.cumsum(zf, axis=1)], axis=1)
    end = cs[:, POOL_BUF + 1:]
    pos = start_pos + jnp.arange(L)
    outs = []
    for gi, w in enumerate(POOL_WINDOWS):
        sl = slice(gi * G_B, (gi + 1) * G_B)
        win = end[:, :, sl] - cs[:, POOL_BUF + 1 - w:POOL_BUF + 1 - w + L, sl]
        cnt = jnp.minimum(w, pos + 1).astype(jnp.float32)[None, :, None]
        outs.append(win / cnt)
    pooled = (jnp.concatenate(outs, axis=-1) - zf[:, POOL_BUF:]).astype(z.dtype)
    y = jnp.einsum('blgc,gcd->blgd', pooled.reshape(b, L, N_POOL_GROUPS, G_B), proj).reshape(b, L, W_B) * scale
    return y, zc[:, -POOL_BUF:]


def diff_attend(q, k, v, mask, lam):
    s = jnp.einsum('bqhmd,bkhmd->bhmqk', q, k, preferred_element_type=jnp.float32) * (DK_C ** -0.5)
    s = jnp.where(mask, s, -jnp.inf)
    p = jax.nn.softmax(s, axis=-1)
    a = p[:, :, 0] - lam * p[:, :, 1]
    return jnp.einsum('bhqk,bkhd->bqhd', a.astype(v.dtype), v)


def diff_attn_prompt(q, k, v, lam):
    b, s = q.shape[:2]
    nb = s // Q_BLOCK
    qb = jnp.moveaxis(q.reshape(b, nb, Q_BLOCK, H_C, 2, DK_C), 1, 0)
    kpos = jnp.arange(s)

    def block(args):
        qi, i = args
        qpos = i * Q_BLOCK + jnp.arange(Q_BLOCK)
        return diff_attend(qi, k, v, qpos[:, None] >= kpos[None, :], lam)

    o = lax.map(block, (qb, jnp.arange(nb)))
    return jnp.moveaxis(o, 0, 1).reshape(b, s, H_C, DV_C)


def wkv_scan(s0, r, w, k, v, kk, a):
    def step(s, inp):
        r_t, w_t, k_t, v_t, kk_t, a_t = inp
        sa = jnp.einsum('bhij,bhj->bhi', s, -kk_t)
        s = s * w_t[:, :, None, :] + sa[..., None] * (kk_t * a_t)[:, :, None, :] + v_t[..., None] * k_t[:, :, None, :]
        return s, jnp.einsum('bhij,bhj->bhi', s, r_t)

    xs = tuple(jnp.moveaxis(t, 1, 0) for t in (r, w, k, v, kk, a))
    s, o = lax.scan(step, s0, xs)
    return jnp.moveaxis(o, 0, 1), s


def rwkv7_mixer(proj, shift_buf, s0, mu, w0, w2, a0, a2, g2, k_k, k_a, r_k, lnx_w, lnx_b):
    b, L, _ = proj.shape
    f32 = jnp.float32
    pj = proj.astype(f32)
    prev = jnp.concatenate([shift_buf.astype(f32), pj[:, :-1]], axis=1)
    xs = pj + (prev - pj) * mu
    r = xs[..., :W_D]
    k = xs[..., W_D:2 * W_D]
    v = xs[..., 2 * W_D:3 * W_D]
    o1 = 3 * W_D
    hw = xs[..., o1:o1 + LORA_W]
    ha = xs[..., o1 + LORA_W:o1 + LORA_W + LORA_A]
    hg = xs[..., o1 + LORA_W + LORA_A:]
    w_log = -jax.nn.softplus(-(w0 + jnp.tanh(hw) @ w2)) - 0.5
    decay = jnp.exp(-jnp.exp(w_log))
    a = jax.nn.sigmoid(a0 + ha @ a2)
    g = jax.nn.sigmoid(hg) @ g2
    hd = lambda t: t.reshape(b, L, H_D, N_D)
    kk = hd(k * k_k)
    kk = kk / jnp.maximum(jnp.sqrt(jnp.sum(kk * kk, axis=-1, keepdims=True)), 1e-12)
    k = k * (1.0 + (a - 1.0) * k_a)
    r4, k4, v4 = hd(r), hd(k), hd(v)
    o, s = wkv_scan(s0.astype(f32), r4, hd(decay), k4, v4, kk, hd(a))
    m = jnp.mean(o, axis=-1, keepdims=True)
    var = jnp.mean(jnp.square(o - m), axis=-1, keepdims=True)
    on = ((o - m) * lax.rsqrt(var + RWKV_GN_EPS)).reshape(b, L, W_D) * lnx_w + lnx_b
    bonus = (jnp.sum(r4 * k4 * r_k, axis=-1, keepdims=True) * v4).reshape(b, L, W_D)
    y = (on + bonus) * g
    return y.astype(proj.dtype), proj[:, -1:], s


def moe_ffn(x, router, w1, w3, w2):
    b, L, d = x.shape
    t = b * L
    xf = x.reshape(t, d)
    logits = (xf @ router).astype(jnp.float32)
    top_logit, top_e = lax.top_k(logits, TOP_K)
    gates = jax.nn.softmax(top_logit, axis=-1)
    tk = t * TOP_K
    flat_e = top_e.reshape(tk)
    flat_tok = jnp.arange(tk) // TOP_K
    flat_g = gates.reshape(tk)
    order = jnp.argsort(flat_e)
    se = flat_e[order]
    counts = jnp.bincount(flat_e, length=N_EXPERTS)
    nblk = (counts + MOE_BLOCK - 1) // MOE_BLOCK
    blk_end = jnp.cumsum(nblk)
    blk_start = blk_end - nblk
    grp_start = jnp.cumsum(counts) - counts
    dest = blk_start[se] * MOE_BLOCK + jnp.arange(tk) - grp_start[se]
    nb = -(-tk // MOE_BLOCK) + N_EXPERTS
    row_tok = jnp.full((nb * MOE_BLOCK,), t, jnp.int32).at[dest].set(flat_tok[order])
    row_g = jnp.zeros((nb * MOE_BLOCK,), jnp.float32).at[dest].set(flat_g[order])
    blk_e = jnp.minimum(jnp.searchsorted(blk_end, jnp.arange(nb), side='right'), N_EXPERTS - 1)
    xpad = jnp.concatenate([xf, jnp.zeros((1, d), xf.dtype)], axis=0)
    xb = xpad[row_tok].reshape(nb, MOE_BLOCK, d)

    def expert_block(args):
        xe, e = args
        return swiglu(xe, w1[e], w3[e], w2[e])

    yb = lax.map(expert_block, (xb, blk_e)).reshape(nb * MOE_BLOCK, d)
    y = jnp.zeros((t + 1, d), jnp.float32).at[row_tok].add(yb.astype(jnp.float32) * row_g[:, None])[:t]
    return y.astype(x.dtype).reshape(b, L, d)


def setup_inputs(seed: int = 0) -> dict:
    key = jax.random.key(seed)
    keys = iter(jax.random.split(key, 64))

    def nrm(shape, scale=1.0):
        return jax.random.normal(next(keys), shape, jnp.float32) * scale

    def gain(shape):
        return 1.0 + 0.02 * nrm(shape)

    n_pages = PAST_LEN // PAGE_SIZE
    n_phys = (DEC_BATCH * n_pages * 5) // 4
    n_even = (DEPTH + 1) // 2
    n_odd = DEPTH // 2
    page_table = jax.random.permutation(next(keys), n_phys)[:DEC_BATCH * n_pages].reshape(DEC_BATCH, n_pages).astype(jnp.int32)
    return dict(
        x_prompt=nrm((BATCH, SEQ, D_MODEL)),
        x_sample=nrm((DEC_BATCH, DEC_SEQ, D_MODEL)),
        cache_k=nrm((DEPTH, n_phys, PAGE_SIZE, H_C, 2, DK_C)),
        cache_v=nrm((DEPTH, n_phys, PAGE_SIZE, H_C, DV_C)),
        state_pool=nrm((DEPTH, DEC_BATCH, POOL_BUF, W_B)),
        state_shift=nrm((DEPTH, DEC_BATCH, 1, N_COLS_D)),
        state_wkv=nrm((DEPTH, DEC_BATCH, H_D, N_D, N_D)),
        page_table=page_table,
        p_prompt=nrm((DEPTH, BATCH, SEQ, PLE_DIM)),
        p_sample=nrm((DEPTH, DEC_BATCH, DEC_SEQ, PLE_DIM)),
        norm_mix_g=gain((DEPTH, D_MODEL)),
        w_in=nrm((DEPTH, D_MODEL, N_IN), D_MODEL ** -0.5),
        w_out=nrm((DEPTH, MIX_W, D_MODEL), MIX_W ** -0.5),
        a_vnorm_g=gain((DEPTH, W_A)),
        a_ws=nrm((DEPTH, H_A, CHUNK, CHUNK), CHUNK ** -0.5),
        a_bs=gain((DEPTH, H_A, CHUNK)),
        b_proj=nrm((DEPTH, N_POOL_GROUPS, G_B, G_B), G_B ** -0.5),
        b_scale=gain((DEPTH, W_B)),
        c_qnorm_g=gain((DEPTH, DK_C)),
        c_knorm_g=gain((DEPTH, DK_C)),
        c_lambda=nrm((DEPTH, 4, DK_C), 0.1),
        c_subln_g=gain((DEPTH, DV_C)),
        d_mu=jax.random.uniform(next(keys), (DEPTH, N_COLS_D), jnp.float32),
        d_w0=nrm((DEPTH, W_D), 0.5),
        d_w2=nrm((DEPTH, LORA_W, W_D), LORA_W ** -0.5),
        d_a0=nrm((DEPTH, W_D), 0.1),
        d_a2=nrm((DEPTH, LORA_A, W_D), LORA_A ** -0.5),
        d_g2=nrm((DEPTH, LORA_G, W_D), LORA_G ** -0.5),
        d_kk=1.0 + 0.1 * nrm((DEPTH, W_D)),
        d_ka=1.0 + 0.1 * nrm((DEPTH, W_D)),
        d_rk=nrm((DEPTH, H_D, N_D), 0.1),
        d_lnx_w=gain((DEPTH, W_D)),
        d_lnx_b=nrm((DEPTH, W_D), 0.02),
        norm_ffn_g=gain((DEPTH, D_MODEL)),
        ffn_w1=nrm((n_even, D_MODEL, D_FF), D_MODEL ** -0.5),
        ffn_w3=nrm((n_even, D_MODEL, D_FF), D_MODEL ** -0.5),
        ffn_w2=nrm((n_even, D_FF, D_MODEL), D_FF ** -0.5),
        moe_router=nrm((n_odd, D_MODEL, N_EXPERTS), D_MODEL ** -0.5),
        moe_w1=nrm((n_odd, N_EXPERTS, D_MODEL, D_FF_EXPERT), D_MODEL ** -0.5),
        moe_w3=nrm((n_odd, N_EXPERTS, D_MODEL, D_FF_EXPERT), D_MODEL ** -0.5),
        moe_w2=nrm((n_odd, N_EXPERTS, D_FF_EXPERT, D_MODEL), D_FF_EXPERT ** -0.5),
        ple_norm_g=gain((DEPTH, D_MODEL)),
        ple_gate=nrm((DEPTH, D_MODEL, D_MODEL), D_MODEL ** -0.5),
        ple_proj=nrm((DEPTH, PLE_DIM, D_MODEL), PLE_DIM ** -0.5),
    )


def reference(x_prompt, x_sample, cache_k, cache_v, state_pool, state_shift, state_wkv, page_table,
              p_prompt, p_sample, norm_mix_g, w_in, w_out, a_vnorm_g, a_ws, a_bs, b_proj, b_scale,
              c_qnorm_g, c_knorm_g, c_lambda, c_subln_g, d_mu, d_w0, d_w2, d_a0, d_a2, d_g2, d_kk, d_ka,
              d_rk, d_lnx_w, d_lnx_b, norm_ffn_g, ffn_w1, ffn_w3, ffn_w2, moe_router, moe_w1, moe_w3,
              moe_w2, ple_norm_g, ple_gate, ple_proj):

    def run(x, p, decode):
        b, L, _ = x.shape
        n_past = page_table.shape[1] * cache_k.shape[2] if decode else 0
        h = x
        ks, vs, pools, shifts, wkvs, chunk_vs = [], [], [], [], [], []
        for l in range(DEPTH):
            z = rmsnorm(h, norm_mix_g[l]) @ w_in[l]
            ya, va = chunk_mlp(z[..., OFF_A:OFF_A + W_A], z[..., OFF_A + W_A:OFF_B], a_vnorm_g[l], a_ws[l], a_bs[l])
            buf = state_pool[l] if decode else jnp.zeros((b, POOL_BUF, W_B), z.dtype)
            yb, pool_new = pool_mixer(z[..., OFF_B:OFF_C], buf, n_past, b_proj[l], b_scale[l])
            q = rmsnorm(z[..., OFF_C:OFF_C + Q_C].reshape(b, L, H_C, 2, DK_C), c_qnorm_g[l])
            k = rmsnorm(z[..., OFF_C + Q_C:OFF_C + 2 * Q_C].reshape(b, L, H_C, 2, DK_C), c_knorm_g[l])
            v = z[..., OFF_C + 2 * Q_C:OFF_D].reshape(b, L, H_C, DV_C)
            lam_init = 0.8 - 0.6 * math.exp(-0.3 * l)
            lp = c_lambda[l].astype(jnp.float32)
            lam = jnp.exp(jnp.sum(lp[0] * lp[1])) - jnp.exp(jnp.sum(lp[2] * lp[3])) + lam_init
            if decode:
                kp = cache_k[l][page_table].reshape(b, n_past, H_C, 2, DK_C)
                vp = cache_v[l][page_table].reshape(b, n_past, H_C, DV_C)
                qpos = n_past + jnp.arange(L)
                kpos = jnp.arange(n_past + L)
                o = diff_attend(q, jnp.concatenate([kp, k], axis=1), jnp.concatenate([vp, v], axis=1),
                                kpos[None, :] <= qpos[:, None], lam)
            else:
                o = diff_attn_prompt(q, k, v, lam)
            yc = (rmsnorm(o, c_subln_g[l]) * (1.0 - lam_init)).reshape(b, L, W_C)
            shift_buf = state_shift[l] if decode else jnp.zeros((b, 1, N_COLS_D), z.dtype)
            s0 = state_wkv[l] if decode else jnp.zeros((b, H_D, N_D, N_D), jnp.float32)
            yd, shift_new, wkv_new = rwkv7_mixer(z[..., OFF_D:], shift_buf, s0, d_mu[l], d_w0[l], d_w2[l],
                                                 d_a0[l], d_a2[l], d_g2[l], d_kk[l], d_ka[l], d_rk[l],
                                                 d_lnx_w[l], d_lnx_b[l])
            h = h + jnp.concatenate([ya, yb, yc, yd], axis=-1) @ w_out[l]
            hn = rmsnorm(h, norm_ffn_g[l])
            if l % 2 == 0:
                h = h + swiglu(hn, ffn_w1[l // 2], ffn_w3[l // 2], ffn_w2[l // 2])
            else:
                h = h + moe_ffn(hn, moe_router[l // 2], moe_w1[l // 2], moe_w3[l // 2], moe_w2[l // 2])
            gate = jax.nn.sigmoid(rmsnorm(h, ple_norm_g[l]) @ ple_gate[l])
            h = h + gate * (p[l] @ ple_proj[l])
            ks.append(k)
            vs.append(v)
            pools.append(pool_new)
            shifts.append(shift_new)
            wkvs.append(wkv_new)
            if decode:
                chunk_vs.append(va)
        return h, jnp.stack(ks), jnp.stack(vs), jnp.stack(pools), jnp.stack(shifts), jnp.stack(wkvs), chunk_vs

    y_prompt, k_pr, v_pr, pool_pr, shift_pr, wkv_pr, _ = run(x_prompt, p_prompt, False)
    y_sample, k_sa, v_sa, pool_sa, shift_sa, wkv_sa, cvs = run(x_sample, p_sample, True)
    chunk_v_sample = jnp.stack(cvs)
    return (y_prompt, y_sample, k_pr, v_pr, pool_pr, shift_pr, wkv_pr,
            k_sa, v_sa, pool_sa, shift_sa, wkv_sa, chunk_v_sample)
```

```python
import functools
import math

import jax
import jax.numpy as jnp
from jax import lax
from jax.experimental import pallas as pl
from jax.experimental.pallas import tpu as pltpu

F32 = jnp.float32
BF16 = jnp.bfloat16

D_MODEL = 1024
DEPTH = 4
HEAD_DIM = 64
W_A = W_B = W_C = W_D = 256
H_A = H_C = H_D = 4
CHUNK = 128
POOL_WINDOWS = (2, 4, 8, 16)
POOL_BUF = 15
POOL_HALO = 16
DK_C = 32
Q_C = 256
LORA_W = 64
LORA_A = 64
LORA_G = 128
N_COLS_D = 3 * W_D + LORA_W + LORA_A + LORA_G
OFF_A = 0
OFF_B = 512
OFF_C = 768
OFF_D = 1536
N_IN = 2560
D_FF = 2816
N_EXPERTS = 8
TOP_K = 2
PLE_DIM = 256
NORM_EPS = 1e-6
RWKV_GN_EPS = 64e-5
NEG_BIG = -1e30

WKV_CHUNK = 64
VMEM_LIMIT = 48 * 1024 * 1024

_NT = (((1,), (1,)), ((), ()))
_TN = (((0,), (0,)), ((), ()))


def _cparams(*sem):
    return pltpu.CompilerParams(dimension_semantics=sem, vmem_limit_bytes=VMEM_LIMIT)


def _bdot(a, b):
    return jnp.dot(a.astype(BF16), b.astype(BF16), preferred_element_type=F32)


def _hdot(a, b, dims=None):
    if dims is None:
        return jnp.dot(a, b, preferred_element_type=F32, precision=lax.Precision.HIGHEST)
    return lax.dot_general(a, b, dims, preferred_element_type=F32, precision=lax.Precision.HIGHEST)


def _gsum(x, gmat):
    hi = x.astype(BF16)
    lo = (x - hi.astype(F32)).astype(BF16)
    return (jnp.dot(hi, gmat, preferred_element_type=F32) + jnp.dot(lo, gmat, preferred_element_type=F32))


def _group_ones(n, g):
    i = jnp.arange(n) // g
    return (i[:, None] == i[None, :]).astype(BF16)


def _full(shape):
    nd = len(shape)
    return pl.BlockSpec(shape, lambda *_: (0,) * nd)


def _inproj_body(h_ref, g_ref, w_ref, gq_ref, gk_ref, g32_ref,
                 za_ref, zb_ref, qn_ref, kn_ref, v_ref, zd_ref, qh_ref, kh_ref, vh_ref):
    x = h_ref[...]
    xn = (x * lax.rsqrt(jnp.mean(x * x, axis=-1, keepdims=True) + NORM_EPS) * g_ref[...]).astype(BF16)

    def proj(lo, hi):
        return jnp.dot(xn, w_ref[:, lo:hi], preferred_element_type=F32)

    za_ref[...] = proj(OFF_A, OFF_B)
    zb_ref[...] = proj(OFF_B, OFF_C)
    zd_ref[...] = proj(OFF_D, N_IN)
    zq = proj(OFF_C, OFF_C + Q_C)
    zk = proj(OFF_C + Q_C, OFF_C + 2 * Q_C)
    zv = proj(OFF_C + 2 * Q_C, OFF_D)
    g32 = g32_ref[...]
    qn = zq * lax.rsqrt(_gsum(zq * zq, g32) * (1.0 / DK_C) + NORM_EPS) * gq_ref[...]
    kn = zk * lax.rsqrt(_gsum(zk * zk, g32) * (1.0 / DK_C) + NORM_EPS) * gk_ref[...]
    qn_ref[...] = qn
    kn_ref[...] = kn
    v_ref[...] = zv
    qs = qn * (DK_C ** -0.5)
    for hh in range(H_C):
        sl = slice(hh * HEAD_DIM, (hh + 1) * HEAD_DIM)
        qh_ref[hh] = qs[:, sl].astype(BF16)
        kh_ref[hh] = kn[:, sl].astype(BF16)
        vh_ref[hh] = zv[:, sl].astype(BF16)


def _inproj(h, g, w_bf, gq, gk, tm):
    t = h.shape[0]
    row = lambda n: pl.BlockSpec((tm, n), lambda i: (i, 0))
    headm = pl.BlockSpec((H_C, tm, HEAD_DIM), lambda i: (0, i, 0))
    f = lambda n: jax.ShapeDtypeStruct((t, n), F32)
    hm = jax.ShapeDtypeStruct((H_C, t, HEAD_DIM), BF16)
    return pl.pallas_call(
        _inproj_body,
        grid=(t // tm,),
        in_specs=[row(D_MODEL), _full((1, D_MODEL)), _full((D_MODEL, N_IN)), _full((1, Q_C)), _full((1, Q_C)),
                  _full((Q_C, Q_C))],
        out_specs=[row(2 * W_A), row(W_B), row(Q_C), row(Q_C), row(W_C), row(N_COLS_D), headm, headm, headm],
        out_shape=[f(2 * W_A), f(W_B), f(Q_C), f(Q_C), f(W_C), f(N_COLS_D), hm, hm, hm],
        compiler_params=_cparams("parallel"),
        name="inproj",
    )(h, g.reshape(1, D_MODEL), w_bf, jnp.tile(gq, 2 * H_C).reshape(1, Q_C), jnp.tile(gk, 2 * H_C).reshape(1, Q_C),
      _group_ones(Q_C, DK_C))


def _mix_a_body(z_ref, vg_ref, g64_ref, wcat_ref, bias_ref, ya_ref, va_ref, *, nchunk):
    z = z_ref[...]
    u = jax.nn.gelu(z[:, :W_A])
    vv = jax.nn.gelu(z[:, W_A:])
    vn = vv * lax.rsqrt(_gsum(vv * vv, g64_ref[...]) * (1.0 / HEAD_DIM) + NORM_EPS) * vg_ref[...]
    va_ref[...] = vn
    head = lax.broadcasted_iota(jnp.int32, (1, W_A), 1) // HEAD_DIM
    wcat = wcat_ref[...]
    bias = bias_ref[...]
    for c in range(nchunk):
        rows = slice(c * CHUNK, (c + 1) * CHUNK)
        vc = vn[rows]
        vstack = jnp.concatenate([jnp.where(head == hh, vc, 0.0) for hh in range(H_A)], axis=0).astype(BF16)
        s = jnp.dot(wcat, vstack, preferred_element_type=F32) + bias
        ya_ref[rows, :] = u[rows] * s


def _mix_a(za, vg, wcat_bf, bias, nchunk):
    t = za.shape[0]
    tm = nchunk * CHUNK
    return pl.pallas_call(
        functools.partial(_mix_a_body, nchunk=nchunk),
        grid=(t // tm,),
        in_specs=[pl.BlockSpec((tm, 2 * W_A), lambda i: (i, 0)), _full((1, W_A)), _full((W_A, W_A)),
                  _full((CHUNK, H_A * CHUNK)), _full((CHUNK, W_A))],
        out_specs=[pl.BlockSpec((tm, W_A), lambda i: (i, 0)), pl.BlockSpec((tm, W_A), lambda i: (i, 0))],
        out_shape=[jax.ShapeDtypeStruct((t, W_A), F32), jax.ShapeDtypeStruct((t, W_A), F32)],
        compiler_params=_cparams("parallel"),
        name="mix_a",
    )(za, vg.reshape(1, W_A), _group_ones(W_A, HEAD_DIM), wcat_bf, bias)


def _mix_b_body(z_ref, buf_ref, proj_ref, scale_ref, y_ref, pool_ref, halo_ref, *, tl, start_pos, nt):
    t = pl.program_id(1)

    @pl.when(t == 0)
    def _():
        halo_ref[...] = buf_ref[0]

    z = z_ref[0]
    x = jnp.concatenate([halo_ref[...], z], axis=0)
    a2 = x + pltpu.roll(x, 1, 0)
    a4 = a2 + pltpu.roll(a2, 2, 0)
    a8 = a4 + pltpu.roll(a4, 4, 0)
    a16 = a8 + pltpu.roll(a8, 8, 0)
    grp = lax.broadcasted_iota(jnp.int32, (1, W_B), 1) // (W_B // len(POOL_WINDOWS))
    win = jnp.where(grp == 0, a2, jnp.where(grp == 1, a4, jnp.where(grp == 2, a8, a16)))[POOL_HALO:]
    wsize = jnp.where(grp == 0, 2, jnp.where(grp == 1, 4, jnp.where(grp == 2, 8, 16)))
    pos = start_pos + t * tl + lax.broadcasted_iota(jnp.int32, (tl, 1), 0)
    cnt = jnp.minimum(wsize, pos + 1).astype(F32)
    pooled = win / cnt - z
    y_ref[0] = _bdot(pooled, proj_ref[...]) * scale_ref[...]
    tail = x[tl:]
    halo_ref[...] = tail

    @pl.when(t == nt - 1)
    def _():
        pool_ref[0] = tail


def _mix_b(zb3, buf16, proj_bd_bf, scale, tl, start_pos):
    nb, L, _ = zb3.shape
    nt = L // tl
    return pl.pallas_call(
        functools.partial(_mix_b_body, tl=tl, start_pos=start_pos, nt=nt),
        grid=(nb, nt),
        in_specs=[pl.BlockSpec((1, tl, W_B), lambda b, t: (b, t, 0)),
                  pl.BlockSpec((1, POOL_HALO, W_B), lambda b, t: (b, 0, 0)),
                  _full((W_B, W_B)), _full((1, W_B))],
        out_specs=[pl.BlockSpec((1, tl, W_B), lambda b, t: (b, t, 0)),
                   pl.BlockSpec((1, POOL_HALO, W_B), lambda b, t: (b, 0, 0))],
        out_shape=[jax.ShapeDtypeStruct((nb, L, W_B), F32), jax.ShapeDtypeStruct((nb, POOL_HALO, W_B), F32)],
        scratch_shapes=[pltpu.VMEM((POOL_HALO, W_B), F32)],
        compiler_params=_cparams("parallel", "arbitrary"),
        name="mix_b",
    )(zb3, buf16, proj_bd_bf, scale.reshape(1, W_B))


def _attn_body(lam_ref, q_ref, k_ref, v_ref, g_ref, o_ref, *, tq, out_scale):
    qi = pl.program_id(2)
    q = q_ref[0]
    lane = lax.broadcasted_iota(jnp.int32, (1, HEAD_DIM), 1)
    zero = jnp.zeros_like(q)
    q0 = jnp.where(lane < DK_C, q, zero)
    q1 = jnp.where(lane >= DK_C, q, zero)

    def soft(s, v, m, l, acc):
        mn = jnp.maximum(m, jnp.max(s, axis=-1, keepdims=True))
        p = jnp.exp(s - mn)
        al = jnp.exp(m - mn)
        return mn, al * l + jnp.sum(p, axis=-1, keepdims=True), al * acc + _bdot(p, v)

    def block(j, carry, masked):
        m0, l0, a0, m1, l1, a1 = carry
        rows = pl.ds(pl.multiple_of(j * tq, tq), tq)
        kb = k_ref[0, rows, :]
        vb = v_ref[0, rows, :]
        s0 = lax.dot_general(q0, kb, _NT, preferred_element_type=F32)
        s1 = lax.dot_general(q1, kb, _NT, preferred_element_type=F32)
        if masked:
            keep = lax.broadcasted_iota(jnp.int32, (tq, tq), 0) >= lax.broadcasted_iota(jnp.int32, (tq, tq), 1)
            s0 = jnp.where(keep, s0, NEG_BIG)
            s1 = jnp.where(keep, s1, NEG_BIG)
        m0, l0, a0 = soft(s0, vb, m0, l0, a0)
        m1, l1, a1 = soft(s1, vb, m1, l1, a1)
        return m0, l0, a0, m1, l1, a1

    col = lambda v: jnp.full((tq, 1), v, F32)
    acc = jnp.zeros((tq, HEAD_DIM), F32)
    carry = (col(NEG_BIG), col(0.0), acc, col(NEG_BIG), col(0.0), acc)
    carry = lax.fori_loop(0, qi, functools.partial(block, masked=False), carry)
    m0, l0, a0, m1, l1, a1 = block(qi, carry, True)
    o = a0 / l0 - lam_ref[0] * (a1 / l1)
    o = o * lax.rsqrt(jnp.mean(o * o, axis=-1, keepdims=True) + NORM_EPS) * g_ref[...] * out_scale
    o_ref[0] = o


def _attn_prompt(lam, qh, kh, vh, g, nb, seq, tq, out_scale):
    nq = seq // tq
    qspec = pl.BlockSpec((1, tq, HEAD_DIM), lambda b, h, i: (h, b * nq + i, 0))
    kspec = pl.BlockSpec((1, seq, HEAD_DIM), lambda b, h, i: (h, b, 0))
    return pl.pallas_call(
        functools.partial(_attn_body, tq=tq, out_scale=out_scale),
        grid=(nb, H_C, nq),
        in_specs=[pl.BlockSpec(memory_space=pltpu.SMEM), qspec, kspec, kspec, _full((1, HEAD_DIM))],
        out_specs=qspec,
        out_shape=jax.ShapeDtypeStruct((H_C, nb * seq, HEAD_DIM), F32),
        compiler_params=_cparams("parallel", "parallel", "parallel"),
        name="attn_prompt",
    )(lam, qh, kh, vh, g.reshape(1, HEAD_DIM))


def _attn_decode_body(pt_ref, lam_ref, q_ref, kn_ref, vn_ref, g_ref, g64_ref, *rest, npages, nq, out_scale):
    del pt_ref
    kp = rest[:npages]
    vp = rest[npages:2 * npages]
    o_ref = rest[2 * npages]
    nrow = 2 * H_C * nq
    q = q_ref[...] * (DK_C ** -0.5)
    row = lax.broadcasted_iota(jnp.int32, (nrow, 1), 0)
    lane = lax.broadcasted_iota(jnp.int32, (1, Q_C), 1)
    qs = jnp.concatenate([q] * (2 * H_C), axis=0)
    qs = jnp.where(row // nq == lane // DK_C, qs, 0.0).astype(BF16)
    s_pages = [lax.dot_general(qs, kp[j][0, 0].astype(BF16), _NT, preferred_element_type=F32)
               for j in range(npages)]
    s_new = lax.dot_general(qs, kn_ref[...].astype(BF16), _NT, preferred_element_type=F32)
    kidx = lax.broadcasted_iota(jnp.int32, (1, nq), 1)
    s_new = jnp.where(row % nq >= kidx, s_new, NEG_BIG)
    m = jnp.max(s_new, axis=-1, keepdims=True)
    for s in s_pages:
        m = jnp.maximum(m, jnp.max(s, axis=-1, keepdims=True))
    p_new = jnp.exp(s_new - m)
    l = jnp.sum(p_new, axis=-1, keepdims=True)
    p_pages = []
    for s in s_pages:
        p = jnp.exp(s - m)
        l = l + jnp.sum(p, axis=-1, keepdims=True)
        p_pages.append(p)
    coef = jnp.where((row // nq) % 2 == 0, 1.0, -lam_ref[0]) / l

    def diff(p):
        pw = p * coef
        return jnp.concatenate([pw[(2 * hh) * nq:(2 * hh + 1) * nq] + pw[(2 * hh + 1) * nq:(2 * hh + 2) * nq]
                                for hh in range(H_C)], axis=0)

    out = _bdot(diff(p_new), vn_ref[...])
    for j in range(npages):
        out = out + _bdot(diff(p_pages[j]), vp[j][0, 0])
    vlane = lax.broadcasted_iota(jnp.int32, (1, W_C), 1) // HEAD_DIM
    o = jnp.zeros((nq, W_C), F32)
    for hh in range(H_C):
        o = jnp.where(vlane == hh, out[hh * nq:(hh + 1) * nq], o)
    o = o * lax.rsqrt(_gsum(o * o, g64_ref[...]) * (1.0 / HEAD_DIM) + NORM_EPS) * g_ref[...] * out_scale
    o_ref[...] = o


def _attn_decode(page_table, lam, qn, kn, vn, g, cache_k4, cache_v4, layer, nq, out_scale):
    nb, npages = page_table.shape
    page = cache_k4.shape[2]
    row = pl.BlockSpec((nq, Q_C), lambda b, pt: (b, 0))

    def page_spec(j):
        return pl.BlockSpec((1, 1, page, Q_C), lambda b, pt, j=j: (layer, pt[b, j], 0, 0))

    grid_spec = pltpu.PrefetchScalarGridSpec(
        num_scalar_prefetch=1,
        grid=(nb,),
        in_specs=[pl.BlockSpec(memory_space=pltpu.SMEM), row, row, row,
                  pl.BlockSpec((1, W_C), lambda b, pt: (0, 0)), pl.BlockSpec((W_C, W_C), lambda b, pt: (0, 0))]
                 + [page_spec(j) for j in range(npages)] * 2,
        out_specs=row,
    )
    return pl.pallas_call(
        functools.partial(_attn_decode_body, npages=npages, nq=nq, out_scale=out_scale),
        grid_spec=grid_spec,
        out_shape=jax.ShapeDtypeStruct((nb * nq, W_C), F32),
        compiler_params=_cparams("parallel"),
        name="attn_decode",
    )(page_table, lam, qn, kn, vn, jnp.tile(g, H_C).reshape(1, W_C), _group_ones(W_C, HEAD_DIM),
      *([cache_k4] * npages), *([cache_v4] * npages))


def _rwkv_prep(x, prev, mu, w0, w2p, a0, a2p, g2, k_k, k_a, g64):
    xs = x + (prev - x) * mu
    r = xs[:, :W_D]
    k = xs[:, W_D:2 * W_D]
    v = xs[:, 2 * W_D:3 * W_D]
    hwa = xs[:, 3 * W_D:3 * W_D + LORA_W + LORA_A]
    hg = xs[:, 3 * W_D + LORA_W + LORA_A:]
    w_log = -jax.nn.softplus(-(w0 + _bdot(jnp.tanh(hwa), w2p))) - 0.5
    logdecay = -jnp.exp(w_log)
    a = jax.nn.sigmoid(a0 + _bdot(hwa, a2p))
    g = _bdot(jax.nn.sigmoid(hg), g2)
    kk = k * k_k
    kk = kk / jnp.maximum(jnp.sqrt(_gsum(kk * kk, g64)), 1e-12)
    k2 = k * (1.0 + (a - 1.0) * k_a)
    return r, logdecay, k2, v, kk, a, g


def _rwkv_post(o, r, k2, v, g, rk, lnw, lnb, g64):
    inv_n = 1.0 / HEAD_DIM
    mean = _gsum(o, g64) * inv_n
    d = o - mean
    var = _gsum(d * d, g64) * inv_n
    on = d * lax.rsqrt(var + RWKV_GN_EPS) * lnw + lnb
    bonus = _gsum(r * k2 * rk, g64) * v
    return (on + bonus) * g


def _wkv_chunk(r, lw, k, v, kk, a, state):
    c = r.shape[0]
    ti = lax.broadcasted_iota(jnp.int32, (c, c), 0)
    si = lax.broadcasted_iota(jnp.int32, (c, c), 1)
    incl = si <= ti
    strict = si < ti
    cw = _hdot(incl.astype(F32), lw)
    tot = cw[c - 1:c]
    e_in = jnp.exp(cw)
    e_out = jnp.exp(-cw)
    e_fut = jnp.exp(tot - cw)
    alpha_b = -kk * jnp.exp(cw - lw)
    r_b = r * e_in
    beta = kk * a
    lane_head = lax.broadcasted_iota(jnp.int32, (1, W_D), 1) // HEAD_DIM
    lhs = jnp.concatenate([jnp.where(lane_head == hh, alpha_b, 0.0) for hh in range(H_D)]
                          + [jnp.where(lane_head == hh, r_b, 0.0) for hh in range(H_D)], axis=0)
    rhs = jnp.concatenate([beta * e_out, k * e_out], axis=0)
    amat = _hdot(lhs, rhs, _NT)
    x0 = _hdot(alpha_b, state, _NT)
    o0 = _hdot(r_b, state, _NT)
    eye = (si == ti).astype(F32)
    u = jnp.zeros((c, W_D), F32)
    o = jnp.zeros((c, W_D), F32)
    for hh in range(H_D):
        ra = slice(hh * c, (hh + 1) * c)
        rr = slice((H_D + hh) * c, (H_D + hh + 1) * c)
        a_ab = jnp.where(strict, amat[ra, :c], 0.0)
        a_ak = jnp.where(strict, amat[ra, c:], 0.0)
        a_rb = jnp.where(incl, amat[rr, :c], 0.0)
        a_rk = jnp.where(incl, amat[rr, c:], 0.0)
        tinv = eye + a_ab
        pw = a_ab
        n = 2
        while n < c:
            pw = _hdot(pw, pw)
            tinv = tinv + _hdot(tinv, pw)
            n *= 2
        uh = _hdot(tinv, x0 + _hdot(a_ak, v))
        oh = o0 + _hdot(jnp.concatenate([a_rb, a_rk], axis=1), jnp.concatenate([uh, v], axis=0))
        mine = lane_head == hh
        u = jnp.where(mine, uh, u)
        o = jnp.where(mine, oh, o)
    upd = _hdot(jnp.concatenate([u, v], axis=0), jnp.concatenate([beta * e_fut, k * e_fut], axis=0), _TN)
    row_head = lax.broadcasted_iota(jnp.int32, (W_D, 1), 0) // HEAD_DIM
    new_state = jnp.where(row_head == lane_head, state * jnp.exp(tot) + upd, 0.0)
    return o, new_state


def _wkv_prompt_body(zd_ref, sb_ref, mu_ref, w0_ref, w2_ref, a0_ref, a2_ref, g2_ref, kk_ref, ka_ref, rk_ref,
                     lnw_ref, lnb_ref, g64_ref, yd_ref, shift_ref, state_ref,
                     st_s, prev_s, r_s, lw_s, k_s, v_s, kk_s, a_s, o_s, *, tt, nt):
    t = pl.program_id(1)

    @pl.when(t == 0)
    def _():
        st_s[...] = jnp.zeros_like(st_s)
        prev_s[...] = sb_ref[0]

    x = zd_ref[0]
    first = lax.broadcasted_iota(jnp.int32, (tt, 1), 0) == 0
    prev = jnp.where(first, prev_s[...], pltpu.roll(x, 1, 0))
    last = x[tt - 1:tt]
    prev_s[...] = last
    g64 = g64_ref[...]
    r, lw, k2, v, kk, a, g = _rwkv_prep(x, prev, mu_ref[...], w0_ref[...], w2_ref[...], a0_ref[...], a2_ref[...],
                                        g2_ref[...], kk_ref[...], ka_ref[...], g64)
    r_s[...] = r
    lw_s[...] = lw
    k_s[...] = k2
    v_s[...] = v
    kk_s[...] = kk
    a_s[...] = a

    def chunk(c, _):
        rows = pl.ds(pl.multiple_of(c * WKV_CHUNK, WKV_CHUNK), WKV_CHUNK)
        o, st = _wkv_chunk(r_s[rows, :], lw_s[rows, :], k_s[rows, :], v_s[rows, :], kk_s[rows, :], a_s[rows, :],
                           st_s[...])
        o_s[rows, :] = o
        st_s[...] = st
        return 0

    lax.fori_loop(0, tt // WKV_CHUNK, chunk, 0)
    yd_ref[0] = _rwkv_post(o_s[...], r, k2, v, g, rk_ref[...], lnw_ref[...], lnb_ref[...], g64)

    @pl.when(t == nt - 1)
    def _():
        shift_ref[0] = last
        state_ref[0] = st_s[...]


def _rwkv_params(mu, w0, w2, a0, a2, g2, k_k, k_a, r_k, lnw, lnb):
    zeros = jnp.zeros((LORA_W, W_D), F32)
    row = lambda x: x.reshape(1, -1)
    return (row(mu), row(w0), jnp.concatenate([w2, zeros], axis=0).astype(BF16), row(a0),
            jnp.concatenate([zeros, a2], axis=0).astype(BF16), g2.astype(BF16), row(k_k), row(k_a), row(r_k),
            row(lnw), row(lnb), _group_ones(W_D, HEAD_DIM))


_RWKV_PARAM_SHAPES = ((1, N_COLS_D), (1, W_D), (LORA_W + LORA_A, W_D), (1, W_D), (LORA_W + LORA_A, W_D),
                      (LORA_G, W_D), (1, W_D), (1, W_D), (1, W_D), (1, W_D), (1, W_D), (W_D, W_D))


def _wkv_prompt(zd3, shift_buf, params, tt):
    nb, L, _ = zd3.shape
    nt = L // tt
    sc = lambda: pltpu.VMEM((tt, W_D), F32)
    return pl.pallas_call(
        functools.partial(_wkv_prompt_body, tt=tt, nt=nt),
        grid=(nb, nt),
        in_specs=[pl.BlockSpec((1, tt, N_COLS_D), lambda b, t: (b, t, 0)),
                  pl.BlockSpec((1, 1, N_COLS_D), lambda b, t: (b, 0, 0))]
                 + [_full(s) for s in _RWKV_PARAM_SHAPES],
        out_specs=[pl.BlockSpec((1, tt, W_D), lambda b, t: (b, t, 0)),
                   pl.BlockSpec((1, 1, N_COLS_D), lambda b, t: (b, 0, 0)),
                   pl.BlockSpec((1, W_D, W_D), lambda b, t: (b, 0, 0))],
        out_shape=[jax.ShapeDtypeStruct((nb, L, W_D), F32), jax.ShapeDtypeStruct((nb, 1, N_COLS_D), F32),
                   jax.ShapeDtypeStruct((nb, W_D, W_D), F32)],
        scratch_shapes=[pltpu.VMEM((W_D, W_D), F32), pltpu.VMEM((1, N_COLS_D), F32)] + [sc() for _ in range(7)],
        compiler_params=_cparams("parallel", "arbitrary"),
        name="wkv_prompt",
    )(zd3, shift_buf, *params)


def _wkv_decode_body(zd_ref, sb_ref, s0_ref, mu_ref, w0_ref, w2_ref, a0_ref, a2_ref, g2_ref, kk_ref, ka_ref, rk_ref,
                     lnw_ref, lnb_ref, g64_ref, yd_ref, shift_ref, state_ref, *, bb, L):
    x3 = zd_ref[...]
    tpos = lax.broadcasted_iota(jnp.int32, (1, L, 1), 1)
    prev3 = jnp.where(tpos == 0, sb_ref[...], pltpu.roll(x3, 1, 1))
    shift_ref[...] = x3[:, L - 1:L, :]
    g64 = g64_ref[...]
    flat = lambda y: y.reshape(bb * L, y.shape[-1])
    r, lw, k2, v, kk, a, g = _rwkv_prep(flat(x3), flat(prev3), mu_ref[...], w0_ref[...], w2_ref[...], a0_ref[...],
                                        a2_ref[...], g2_ref[...], kk_ref[...], ka_ref[...], g64)
    cube = lambda y: y.reshape(bb, L, W_D)
    r3, w3, k3, v3, kk3, b3 = cube(r), cube(jnp.exp(lw)), cube(k2), cube(v), cube(kk), cube(kk * a)
    eye = (lax.broadcasted_iota(jnp.int32, (HEAD_DIM, HEAD_DIM), 0)
           == lax.broadcasted_iota(jnp.int32, (HEAD_DIM, HEAD_DIM), 1)).astype(F32)[None]
    outs = [[None] * H_D for _ in range(L)]
    for hh in range(H_D):
        sl = slice(hh * HEAD_DIM, (hh + 1) * HEAD_DIM)
        s = s0_ref[:, hh]
        for t in range(L):
            tok = lambda y: y[:, t:t + 1, sl]
            sa = jnp.sum(s * -tok(kk3), axis=-1, keepdims=True)
            vcol = jnp.sum(eye * tok(v3), axis=-1, keepdims=True)
            s = s * tok(w3) + sa * tok(b3) + vcol * tok(k3)
            ocol = jnp.sum(s * tok(r3), axis=-1, keepdims=True)
            outs[t][hh] = jnp.sum(eye * ocol, axis=1, keepdims=True)
        state_ref[:, hh] = s
    o3 = jnp.concatenate([jnp.concatenate(row, axis=-1) for row in outs], axis=1)
    y = _rwkv_post(flat(o3), r, k2, v, g, rk_ref[...], lnw_ref[...], lnb_ref[...], g64)
    yd_ref[...] = y.reshape(bb, L, W_D)


def _wkv_decode(zd3, shift_buf, s0, params, bb):
    nb, L, _ = zd3.shape
    blk = lambda *s: pl.BlockSpec((bb,) + s, lambda i: (i,) + (0,) * len(s))
    return pl.pallas_call(
        functools.partial(_wkv_decode_body, bb=bb, L=L),
        grid=(nb // bb,),
        in_specs=[blk(L, N_COLS_D), blk(1, N_COLS_D), blk(H_D, HEAD_DIM, HEAD_DIM)]
                 + [_full(s) for s in _RWKV_PARAM_SHAPES],
        out_specs=[blk(L, W_D), blk(1, N_COLS_D), blk(H_D, HEAD_DIM, HEAD_DIM)],
        out_shape=[jax.ShapeDtypeStruct((nb, L, W_D), F32), jax.ShapeDtypeStruct((nb, 1, N_COLS_D), F32),
                   jax.ShapeDtypeStruct((nb, H_D, HEAD_DIM, HEAD_DIM), F32)],
        compiler_params=_cparams("parallel"),
        name="wkv_decode",
    )(zd3, shift_buf, s0, *params)


def _outproj_body(h_ref, ya_ref, yb_ref, yc_ref, yd_ref, w_ref, o_ref, *, head_major):
    acc = h_ref[...]
    acc = acc + _bdot(ya_ref[...], w_ref[0:W_A, :])
    acc = acc + _bdot(yb_ref[...], w_ref[W_A:W_A + W_B, :])
    off = W_A + W_B
    if head_major:
        for hh in range(H_C):
            acc = acc + _bdot(yc_ref[hh], w_ref[off + hh * HEAD_DIM:off + (hh + 1) * HEAD_DIM, :])
    else:
        acc = acc + _bdot(yc_ref[...], w_ref[off:off + W_C, :])
    acc = acc + _bdot(yd_ref[...], w_ref[off + W_C:, :])
    o_ref[...] = acc


def _outproj(h, ya, yb, yc, yd, w_bf, tm, head_major):
    t = h.shape[0]
    row = lambda n: pl.BlockSpec((tm, n), lambda i: (i, 0))
    ycs = pl.BlockSpec((H_C, tm, HEAD_DIM), lambda i: (0, i, 0)) if head_major else row(W_C)
    return pl.pallas_call(
        functools.partial(_outproj_body, head_major=head_major),
        grid=(t // tm,),
        in_specs=[row(D_MODEL), row(W_A), row(W_B), ycs, row(W_D), _full((D_MODEL, D_MODEL))],
        out_specs=row(D_MODEL),
        out_shape=jax.ShapeDtypeStruct((t, D_MODEL), F32),
        compiler_params=_cparams("parallel"),
        name="outproj",
    )(h, ya, yb, yc, yd, w_bf)


def _ffn_body(blk_ref, h_ref, g_ref, w1_ref, w3_ref, w2_ref, o_ref, hn_s, acc_s, *, residual):
    del blk_ref
    j = pl.program_id(1)

    @pl.when(j == 0)
    def _():
        x = h_ref[...]
        hn_s[...] = (x * lax.rsqrt(jnp.mean(x * x, axis=-1, keepdims=True) + NORM_EPS) * g_ref[...]).astype(BF16)
        acc_s[...] = x if residual else jnp.zeros_like(x)

    hn = hn_s[...]
    a = jnp.dot(hn, w1_ref[0], preferred_element_type=F32)
    b = jnp.dot(hn, w3_ref[0], preferred_element_type=F32)
    acc_s[...] += _bdot(jax.nn.silu(a) * b, w2_ref[0])

    @pl.when(j == pl.num_programs(1) - 1)
    def _():
        o_ref[...] = acc_s[...]


def _ffn(blk_e, x, g, w1, w3, w2, tm, tf, residual):
    t = x.shape[0]
    ff = w1.shape[2]
    grid_spec = pltpu.PrefetchScalarGridSpec(
        num_scalar_prefetch=1,
        grid=(t // tm, ff // tf),
        in_specs=[pl.BlockSpec((tm, D_MODEL), lambda i, j, e: (i, 0)),
                  pl.BlockSpec((1, D_MODEL), lambda i, j, e: (0, 0)),
                  pl.BlockSpec((1, D_MODEL, tf), lambda i, j, e: (e[i], 0, j)),
                  pl.BlockSpec((1, D_MODEL, tf), lambda i, j, e: (e[i], 0, j)),
                  pl.BlockSpec((1, tf, D_MODEL), lambda i, j, e: (e[i], j, 0))],
        out_specs=pl.BlockSpec((tm, D_MODEL), lambda i, j, e: (i, 0)),
        scratch_shapes=[pltpu.VMEM((tm, D_MODEL), BF16), pltpu.VMEM((tm, D_MODEL), F32)],
    )
    return pl.pallas_call(
        functools.partial(_ffn_body, residual=residual),
        grid_spec=grid_spec,
        out_shape=jax.ShapeDtypeStruct((t, D_MODEL), F32),
        compiler_params=_cparams("parallel", "arbitrary"),
        name="ffn" if residual else "moe_ffn",
    )(blk_e, x, g.reshape(1, D_MODEL), w1, w3, w2)


def _router_body(h_ref, g_ref, w_ref, lg_ref):
    x = h_ref[...]
    hn = x * lax.rsqrt(jnp.mean(x * x, axis=-1, keepdims=True) + NORM_EPS) * g_ref[...]
    lg_ref[...] = _hdot(hn, w_ref[...])


def _router(h, g, w_pad, tm):
    t = h.shape[0]
    row = lambda n: pl.BlockSpec((tm, n), lambda i: (i, 0))
    return pl.pallas_call(
        _router_body,
        grid=(t // tm,),
        in_specs=[row(D_MODEL), _full((1, D_MODEL)), _full((D_MODEL, 128))],
        out_specs=row(128),
        out_shape=jax.ShapeDtypeStruct((t, 128), F32),
        compiler_params=_cparams("parallel"),
        name="router",
    )(h, g.reshape(1, D_MODEL), w_pad)


def _moe(h, g, router, w1, w3, w2, tm, tf):
    t = h.shape[0]
    logits = _router(h, g, jnp.pad(router, ((0, 0), (0, 128 - N_EXPERTS))), tm)
    top_logit, top_e = lax.top_k(logits[:, :N_EXPERTS], TOP_K)
    gates = jax.nn.softmax(top_logit, axis=-1)
    tk = t * TOP_K
    flat_e = top_e.reshape(tk)
    onehot = (flat_e[:, None] == jnp.arange(N_EXPERTS)[None, :]).astype(jnp.int32)
    rank = jnp.take_along_axis(jnp.cumsum(onehot, axis=0) - onehot, flat_e[:, None], axis=1)[:, 0]
    nblk = (jnp.sum(onehot, axis=0) + tm - 1) // tm
    blk_end = jnp.cumsum(nblk)
    dest = (blk_end - nblk)[flat_e] * tm + rank
    nb = -(-tk // tm) + N_EXPERTS
    row_tok = jnp.full((nb * tm,), t, jnp.int32).at[dest].set(jnp.arange(tk, dtype=jnp.int32) // TOP_K)
    blk_e = jnp.minimum(jnp.searchsorted(blk_end, jnp.arange(nb), side='right'), N_EXPERTS - 1).astype(jnp.int32)
    xg = jnp.concatenate([h, jnp.zeros((1, D_MODEL), F32)], axis=0)[row_tok]
    yb = _ffn(blk_e, xg, g, w1, w3, w2, tm, tf, False)
    d2 = dest.reshape(t, TOP_K)
    return h + (yb[d2[:, 0]] * gates[:, 0:1] + yb[d2[:, 1]] * gates[:, 1:2])


def _ple_body(h_ref, g_ref, wg_ref, p_ref, wp_ref, o_ref):
    x = h_ref[...]
    hn = (x * lax.rsqrt(jnp.mean(x * x, axis=-1, keepdims=True) + NORM_EPS) * g_ref[...]).astype(BF16)
    gate = jax.nn.sigmoid(jnp.dot(hn, wg_ref[...], preferred_element_type=F32))
    o_ref[...] = x + gate * _bdot(p_ref[...], wp_ref[...])


def _ple(h, g, wg_bf, p, wp_bf, tm):
    t = h.shape[0]
    row = lambda n: pl.BlockSpec((tm, n), lambda i: (i, 0))
    return pl.pallas_call(
        _ple_body,
        grid=(t // tm,),
        in_specs=[row(D_MODEL), _full((1, D_MODEL)), _full((D_MODEL, D_MODEL)), row(PLE_DIM),
                  _full((PLE_DIM, D_MODEL))],
        out_specs=row(D_MODEL),
        out_shape=jax.ShapeDtypeStruct((t, D_MODEL), F32),
        compiler_params=_cparams("parallel"),
        name="ple",
    )(h, g.reshape(1, D_MODEL), wg_bf, p, wp_bf)


def _row_tile(t, want):
    tm = min(t, want)
    assert t % tm == 0, (t, tm)
    return tm


def _run(x, p, decode, wts, cache, state_pool, state_shift, state_wkv, page_table):
    nb, L, _ = x.shape
    t = nb * L
    tm = _row_tile(t, 512)
    n_past = page_table.shape[1] * cache[0].shape[2] if decode else 0
    h = x.reshape(t, D_MODEL)
    ks, vs, pools, shifts, wkvs, chunk_vs = [], [], [], [], [], []
    causal = jnp.tril(jnp.ones((CHUNK, CHUNK), bool))
    for l in range(DEPTH):
        w = wts[l]
        za, zb, qn, kn, v, zd, qh, kh, vh = _inproj(h, w["norm_mix_g"], w["w_in"], w["c_qnorm_g"], w["c_knorm_g"], tm)

        ws = jnp.where(causal, w["a_ws"], 0.0)
        bs = w["a_bs"]
        if decode:
            reps = CHUNK // L
            ws = jax.vmap(lambda m: jnp.kron(jnp.eye(reps, dtype=F32), m[:L, :L]))(ws)
            bs = jnp.tile(bs[:, :L], (1, reps))
        wcat = jnp.concatenate([ws[hh] for hh in range(H_A)], axis=1).astype(BF16)
        bias = jnp.repeat(bs.T, HEAD_DIM, axis=1)
        ya, va = _mix_a(za, w["a_vnorm_g"], wcat, bias, _row_tile(t, 512) // CHUNK)

        if decode:
            buf16 = jnp.pad(state_pool[l], ((0, 0), (POOL_HALO - POOL_BUF, 0), (0, 0)))
        else:
            buf16 = jnp.zeros((nb, POOL_HALO, W_B), F32)
        proj_bd = jax.scipy.linalg.block_diag(*[w["b_proj"][gi] for gi in range(len(POOL_WINDOWS))]).astype(BF16)
        yb, pool16 = _mix_b(zb.reshape(nb, L, W_B), buf16, proj_bd, w["b_scale"], _row_tile(L, 512), n_past)
        yb = yb.reshape(t, W_B)

        lam_init = 0.8 - 0.6 * math.exp(-0.3 * l)
        lp = w["c_lambda"]
        lam = (jnp.exp(jnp.sum(lp[0] * lp[1])) - jnp.exp(jnp.sum(lp[2] * lp[3])) + lam_init).reshape(1)
        if decode:
            yc = _attn_decode(page_table, lam, qn, kn, v, w["c_subln_g"], cache[0], cache[1], l, L, 1.0 - lam_init)
        else:
            yc = _attn_prompt(lam, qh, kh, vh, w["c_subln_g"], nb, L, _row_tile(L, 256), 1.0 - lam_init)

        params = _rwkv_params(w["d_mu"], w["d_w0"], w["d_w2"], w["d_a0"], w["d_a2"], w["d_g2"], w["d_kk"], w["d_ka"],
                              w["d_rk"].reshape(-1), w["d_lnx_w"], w["d_lnx_b"])
        zd3 = zd.reshape(nb, L, N_COLS_D)
        if decode:
            yd, shift_new, wkv_new = _wkv_decode(zd3, state_shift[l], state_wkv[l], params, 8)
        else:
            yd, shift_new, st = _wkv_prompt(zd3, jnp.zeros((nb, 1, N_COLS_D), F32), params, _row_tile(L, 512))
            st = st.reshape(nb, H_D, HEAD_DIM, H_D, HEAD_DIM)
            wkv_new = jnp.stack([st[:, hh, :, hh, :] for hh in range(H_D)], axis=1)
        h = _outproj(h, ya, yb, yc, yd.reshape(t, W_D), w["w_out"], tm, head_major=not decode)

        if l % 2 == 0:
            h = _ffn(jnp.zeros((t // tm,), jnp.int32), h, w["norm_ffn_g"], w["ffn_w1"], w["ffn_w3"], w["ffn_w2"],
                     tm, D_FF // 2, True)
        else:
            h = _moe(h, w["norm_ffn_g"], w["moe_router"], w["moe_w1"], w["moe_w3"], w["moe_w2"], tm, D_FF // 2)

        h = _ple(h, w["ple_norm_g"], w["ple_gate"], p[l].reshape(t, PLE_DIM), w["ple_proj"], tm)

        ks.append(kn.reshape(nb, L, H_C, 2, DK_C))
        vs.append(v.reshape(nb, L, H_C, HEAD_DIM))
        pools.append(pool16[:, POOL_HALO - POOL_BUF:])
        shifts.append(shift_new)
        wkvs.append(wkv_new)
        chunk_vs.append(va.reshape(nb, L, W_A))
    return (h.reshape(nb, L, D_MODEL), jnp.stack(ks), jnp.stack(vs), jnp.stack(pools), jnp.stack(shifts),
            jnp.stack(wkvs), jnp.stack(chunk_vs))


def kernel(x_prompt, x_sample, cache_k, cache_v, state_pool, state_shift, state_wkv, page_table, p_prompt, p_sample, norm_mix_g, w_in, w_out, a_vnorm_g, a_ws, a_bs, b_proj, b_scale, c_qnorm_g, c_knorm_g, c_lambda, c_subln_g, d_mu, d_w0, d_w2, d_a0, d_a2, d_g2, d_kk, d_ka, d_rk, d_lnx_w, d_lnx_b, norm_ffn_g, ffn_w1, ffn_w3, ffn_w2, moe_router, moe_w1, moe_w3, moe_w2, ple_norm_g, ple_gate, ple_proj):
    bf = lambda a: a.astype(BF16)
    wts = []
    for l in range(DEPTH):
        w = dict(norm_mix_g=norm_mix_g[l], w_in=bf(w_in[l]), w_out=bf(w_out[l]), a_vnorm_g=a_vnorm_g[l],
                 a_ws=a_ws[l], a_bs=a_bs[l], b_proj=b_proj[l], b_scale=b_scale[l], c_qnorm_g=c_qnorm_g[l],
                 c_knorm_g=c_knorm_g[l], c_lambda=c_lambda[l], c_subln_g=c_subln_g[l], d_mu=d_mu[l], d_w0=d_w0[l],
                 d_w2=d_w2[l], d_a0=d_a0[l], d_a2=d_a2[l], d_g2=d_g2[l], d_kk=d_kk[l], d_ka=d_ka[l], d_rk=d_rk[l],
                 d_lnx_w=d_lnx_w[l], d_lnx_b=d_lnx_b[l], norm_ffn_g=norm_ffn_g[l], ple_norm_g=ple_norm_g[l],
                 ple_gate=bf(ple_gate[l]), ple_proj=bf(ple_proj[l]))
        if l % 2 == 0:
            w.update(ffn_w1=bf(ffn_w1[l // 2])[None], ffn_w3=bf(ffn_w3[l // 2])[None], ffn_w2=bf(ffn_w2[l // 2])[None])
        else:
            w.update(moe_router=moe_router[l // 2], moe_w1=bf(moe_w1[l // 2]), moe_w3=bf(moe_w3[l // 2]),
                     moe_w2=bf(moe_w2[l // 2]))
        wts.append(w)
    depth, n_phys, page = cache_k.shape[:3]
    cache = (cache_k.reshape(depth, n_phys, page, Q_C), cache_v.reshape(depth, n_phys, page, W_C))
    y_p, k_p, v_p, pool_p, shift_p, wkv_p, _ = _run(x_prompt, p_prompt, False, wts, cache, state_pool, state_shift,
                                                    state_wkv, page_table)
    y_s, k_s, v_s, pool_s, shift_s, wkv_s, cv_s = _run(x_sample, p_sample, True, wts, cache, state_pool, state_shift,
                                                       state_wkv, page_table)
    return (y_p, y_s, k_p, v_p, pool_p, shift_p, wkv_p, k_s, v_s, pool_s, shift_s, wkv_s, cv_s)
```

```python
import functools
import math

import jax
import jax.numpy as jnp
from jax import lax
from jax.experimental import pallas as pl
from jax.experimental.pallas import tpu as pltpu

F32 = jnp.float32
BF16 = jnp.bfloat16

D_MODEL = 1024
DEPTH = 4
HEAD_DIM = 64
W_A = W_B = W_C = W_D = 256
H_A = H_C = H_D = 4
CHUNK = 128
POOL_WINDOWS = (2, 4, 8, 16)
POOL_BUF = 15
POOL_HALO = 16
DK_C = 32
Q_C = 256
LORA_W = 64
LORA_A = 64
LORA_G = 128
N_COLS_D = 3 * W_D + LORA_W + LORA_A + LORA_G
OFF_A = 0
OFF_B = 512
OFF_C = 768
OFF_D = 1536
N_IN = 2560
D_FF = 2816
N_EXPERTS = 8
TOP_K = 2
PLE_DIM = 256
NORM_EPS = 1e-6
RWKV_GN_EPS = 64e-5
NEG_BIG = -1e30
LOG2_E = 1.4426950408889634

ATTN_TQ, ATTN_TK, ATTN_NSUB = 1024, 1024, 2
WKV_GROUP = 2
WKV_CHUNK = 64
VMEM_LIMIT = 48 * 1024 * 1024

_NT = (((1,), (1,)), ((), ()))
_TN = (((0,), (0,)), ((), ()))


def _cparams(*sem):
    return pltpu.CompilerParams(dimension_semantics=sem, vmem_limit_bytes=VMEM_LIMIT)


def _bdot(a, b):
    return jnp.dot(a.astype(BF16), b.astype(BF16), preferred_element_type=F32)


def _hdot(a, b, dims=None):
    if dims is None:
        return jnp.dot(a, b, preferred_element_type=F32, precision=lax.Precision.HIGHEST)
    return lax.dot_general(a, b, dims, preferred_element_type=F32, precision=lax.Precision.HIGHEST)


def _gsum(x, gmat):
    hi = x.astype(BF16)
    lo = (x - hi.astype(F32)).astype(BF16)
    return (jnp.dot(hi, gmat, preferred_element_type=F32) + jnp.dot(lo, gmat, preferred_element_type=F32))


def _group_ones(n, g):
    i = jnp.arange(n) // g
    return (i[:, None] == i[None, :]).astype(BF16)


def _full(shape):
    nd = len(shape)
    return pl.BlockSpec(shape, lambda *_: (0,) * nd)


def _inproj_body(h_ref, g_ref, w_ref, gq_ref, gk_ref, g32_ref,
                 za_ref, zb_ref, qn_ref, kn_ref, v_ref, zd_ref, qh_ref, kh_ref, vt_ref):
    x = h_ref[...]
    xn = (x * lax.rsqrt(jnp.mean(x * x, axis=-1, keepdims=True) + NORM_EPS) * g_ref[...]).astype(BF16)

    def proj(lo, hi):
        return jnp.dot(xn, w_ref[:, lo:hi], preferred_element_type=F32)

    za_ref[...] = proj(OFF_A, OFF_B)
    zb_ref[...] = proj(OFF_B, OFF_C)
    zd_ref[...] = proj(OFF_D, N_IN)
    zq = proj(OFF_C, OFF_C + Q_C)
    zk = proj(OFF_C + Q_C, OFF_C + 2 * Q_C)
    zv = proj(OFF_C + 2 * Q_C, OFF_D)
    g32 = g32_ref[...]
    qn = zq * lax.rsqrt(_gsum(zq * zq, g32) * (1.0 / DK_C) + NORM_EPS) * gq_ref[...]
    kn = zk * lax.rsqrt(_gsum(zk * zk, g32) * (1.0 / DK_C) + NORM_EPS) * gk_ref[...]
    qn_ref[...] = qn
    kn_ref[...] = kn
    v_ref[...] = zv
    qs = qn * (DK_C ** -0.5 * LOG2_E)
    for hh in range(H_C):
        sl = slice(hh * HEAD_DIM, (hh + 1) * HEAD_DIM)
        qh_ref[hh] = qs[:, sl].astype(BF16)
        kh_ref[hh] = kn[:, sl].astype(BF16)
    vt_ref[...] = zv.T.astype(BF16)


def _inproj(h, g, w_bf, gq, gk, tm):
    t = h.shape[0]
    row = lambda n: pl.BlockSpec((tm, n), lambda i: (i, 0))
    headm = pl.BlockSpec((H_C, tm, HEAD_DIM), lambda i: (0, i, 0))
    f = lambda n: jax.ShapeDtypeStruct((t, n), F32)
    hm = jax.ShapeDtypeStruct((H_C, t, HEAD_DIM), BF16)
    return pl.pallas_call(
        _inproj_body,
        grid=(t // tm,),
        in_specs=[row(D_MODEL), _full((1, D_MODEL)), _full((D_MODEL, N_IN)), _full((1, Q_C)), _full((1, Q_C)),
                  _full((Q_C, Q_C))],
        out_specs=[row(2 * W_A), row(W_B), row(Q_C), row(Q_C), row(W_C), row(N_COLS_D), headm, headm,
                   pl.BlockSpec((W_C, tm), lambda i: (0, i))],
        out_shape=[f(2 * W_A), f(W_B), f(Q_C), f(Q_C), f(W_C), f(N_COLS_D), hm, hm,
                   jax.ShapeDtypeStruct((W_C, t), BF16)],
        compiler_params=_cparams("parallel"),
        name="inproj",
    )(h, g.reshape(1, D_MODEL), w_bf, jnp.tile(gq, 2 * H_C).reshape(1, Q_C), jnp.tile(gk, 2 * H_C).reshape(1, Q_C),
      _group_ones(Q_C, DK_C))


def _mix_a_body(z_ref, vg_ref, g64_ref, wcat_ref, bias_ref, ya_ref, va_ref, *, nchunk):
    z = z_ref[...]
    u = jax.nn.gelu(z[:, :W_A])
    vv = jax.nn.gelu(z[:, W_A:])
    vn = vv * lax.rsqrt(_gsum(vv * vv, g64_ref[...]) * (1.0 / HEAD_DIM) + NORM_EPS) * vg_ref[...]
    va_ref[...] = vn
    head = lax.broadcasted_iota(jnp.int32, (1, W_A), 1) // HEAD_DIM
    wcat = wcat_ref[...]
    bias = bias_ref[...]
    for c in range(nchunk):
        rows = slice(c * CHUNK, (c + 1) * CHUNK)
        vc = vn[rows]
        vstack = jnp.concatenate([jnp.where(head == hh, vc, 0.0) for hh in range(H_A)], axis=0).astype(BF16)
        s = jnp.dot(wcat, vstack, preferred_element_type=F32) + bias
        ya_ref[rows, :] = u[rows] * s


def _mix_a(za, vg, wcat_bf, bias, nchunk):
    t = za.shape[0]
    tm = nchunk * CHUNK
    return pl.pallas_call(
        functools.partial(_mix_a_body, nchunk=nchunk),
        grid=(t // tm,),
        in_specs=[pl.BlockSpec((tm, 2 * W_A), lambda i: (i, 0)), _full((1, W_A)), _full((W_A, W_A)),
                  _full((CHUNK, H_A * CHUNK)), _full((CHUNK, W_A))],
        out_specs=[pl.BlockSpec((tm, W_A), lambda i: (i, 0)), pl.BlockSpec((tm, W_A), lambda i: (i, 0))],
        out_shape=[jax.ShapeDtypeStruct((t, W_A), F32), jax.ShapeDtypeStruct((t, W_A), F32)],
        compiler_params=_cparams("parallel"),
        name="mix_a",
    )(za, vg.reshape(1, W_A), _group_ones(W_A, HEAD_DIM), wcat_bf, bias)


def _mix_b_body(z_ref, buf_ref, proj_ref, scale_ref, y_ref, pool_ref, halo_ref, *, tl, start_pos, nt):
    t = pl.program_id(1)

    @pl.when(t == 0)
    def _():
        halo_ref[...] = buf_ref[0]

    z = z_ref[0]
    x = jnp.concatenate([halo_ref[...], z], axis=0)
    a2 = x + pltpu.roll(x, 1, 0)
    a4 = a2 + pltpu.roll(a2, 2, 0)
    a8 = a4 + pltpu.roll(a4, 4, 0)
    a16 = a8 + pltpu.roll(a8, 8, 0)
    grp = lax.broadcasted_iota(jnp.int32, (1, W_B), 1) // (W_B // len(POOL_WINDOWS))
    win = jnp.where(grp == 0, a2, jnp.where(grp == 1, a4, jnp.where(grp == 2, a8, a16)))[POOL_HALO:]
    wsize = jnp.where(grp == 0, 2, jnp.where(grp == 1, 4, jnp.where(grp == 2, 8, 16)))
    pos = start_pos + t * tl + lax.broadcasted_iota(jnp.int32, (tl, 1), 0)
    cnt = jnp.minimum(wsize, pos + 1).astype(F32)
    pooled = win / cnt - z
    y_ref[0] = _bdot(pooled, proj_ref[...]) * scale_ref[...]
    tail = x[tl:]
    halo_ref[...] = tail

    @pl.when(t == nt - 1)
    def _():
        pool_ref[0] = tail


def _mix_b(zb3, buf16, proj_bd_bf, scale, tl, start_pos):
    nb, L, _ = zb3.shape
    nt = L // tl
    return pl.pallas_call(
        functools.partial(_mix_b_body, tl=tl, start_pos=start_pos, nt=nt),
        grid=(nb, nt),
        in_specs=[pl.BlockSpec((1, tl, W_B), lambda b, t: (b, t, 0)),
                  pl.BlockSpec((1, POOL_HALO, W_B), lambda b, t: (b, 0, 0)),
                  _full((W_B, W_B)), _full((1, W_B))],
        out_specs=[pl.BlockSpec((1, tl, W_B), lambda b, t: (b, t, 0)),
                   pl.BlockSpec((1, POOL_HALO, W_B), lambda b, t: (b, 0, 0))],
        out_shape=[jax.ShapeDtypeStruct((nb, L, W_B), F32), jax.ShapeDtypeStruct((nb, POOL_HALO, W_B), F32)],
        scratch_shapes=[pltpu.VMEM((POOL_HALO, W_B), F32)],
        compiler_params=_cparams("parallel", "arbitrary"),
        name="mix_b",
    )(zb3, buf16, proj_bd_bf, scale.reshape(1, W_B))


def _attn_body(lam_ref, q_ref, k_ref, vt_ref, g_ref, o_ref, *, tq, tk, nsub, out_scale):
    qi = pl.program_id(2)
    q = q_ref[0]
    lane = lax.broadcasted_iota(jnp.int32, (1, HEAD_DIM), 1)
    zero = jnp.zeros_like(q)
    q0 = jnp.where(lane < DK_C, q, zero)
    q1 = jnp.where(lane >= DK_C, q, zero)

    def soft(s, vt, m, l, acc):
        mn = jnp.maximum(m, jnp.max(s, axis=0, keepdims=True))
        p = jnp.exp2(s - mn)
        al = jnp.exp2(m - mn)
        return mn, al * l + jnp.sum(p, axis=0, keepdims=True), al * acc + jnp.dot(
            vt, p.astype(BF16), preferred_element_type=F32)

    ts = tk // nsub

    def block(j, carry, masked):
        m0, l0, a0, m1, l1, a1 = carry
        for u in range(nsub):
            off = pl.multiple_of(j * tk + u * ts, ts)
            kb = k_ref[0, pl.ds(off, ts), :]
            vt = vt_ref[:, pl.ds(off, ts)]
            s0 = lax.dot_general(kb, q0, _NT, preferred_element_type=F32)
            s1 = lax.dot_general(kb, q1, _NT, preferred_element_type=F32)
            if masked:
                key = off + lax.broadcasted_iota(jnp.int32, (ts, tq), 0)
                qry = qi * tq + lax.broadcasted_iota(jnp.int32, (ts, tq), 1)
                s0 = jnp.where(key <= qry, s0, NEG_BIG)
                s1 = jnp.where(key <= qry, s1, NEG_BIG)
            m0, l0, a0 = soft(s0, vt, m0, l0, a0)
            m1, l1, a1 = soft(s1, vt, m1, l1, a1)
        return m0, l0, a0, m1, l1, a1

    rowv = lambda v: jnp.full((1, tq), v, F32)
    acc = jnp.zeros((HEAD_DIM, tq), F32)
    carry = (rowv(NEG_BIG), rowv(0.0), acc, rowv(NEG_BIG), rowv(0.0), acc)
    nfull = (qi * tq) // tk
    carry = lax.fori_loop(0, nfull, functools.partial(block, masked=False), carry)
    m0, l0, a0, m1, l1, a1 = block(nfull, carry, True)
    o = a0 / l0 - lam_ref[0] * (a1 / l1)
    o_ref[...] = o * lax.rsqrt(jnp.mean(o * o, axis=0, keepdims=True) + NORM_EPS) * g_ref[...] * out_scale


def _attn_prompt(lam, qh, kh, vt, g, nb, seq, tq, tk, nsub, out_scale):
    nq = seq // tq
    assert tk % tq == 0 and seq % tk == 0 and tk % nsub == 0, (seq, tq, tk, nsub)
    qspec = pl.BlockSpec((1, tq, HEAD_DIM), lambda b, h, i: (h, b * nq + i, 0))
    kspec = pl.BlockSpec((1, seq, HEAD_DIM), lambda b, h, i: (h, b, 0))
    vspec = pl.BlockSpec((HEAD_DIM, seq), lambda b, h, i: (h, b))
    return pl.pallas_call(
        functools.partial(_attn_body, tq=tq, tk=tk, nsub=nsub, out_scale=out_scale),
        grid=(nb, H_C, nq),
        in_specs=[pl.BlockSpec(memory_space=pltpu.SMEM), qspec, kspec, vspec, _full((HEAD_DIM, 1))],
        out_specs=pl.BlockSpec((HEAD_DIM, tq), lambda b, h, i: (h, b * nq + i)),
        out_shape=jax.ShapeDtypeStruct((W_C, nb * seq), F32),
        compiler_params=_cparams("parallel", "parallel", "parallel"),
        name="attn_prompt",
    )(lam, qh, kh, vt, g.reshape(HEAD_DIM, 1))


def _attn_decode_body(pt_ref, lam_ref, q_ref, kn_ref, vn_ref, g_ref, g64_ref, *rest, npages, nq, out_scale):
    del pt_ref
    kp = rest[:npages]
    vp = rest[npages:2 * npages]
    o_ref = rest[2 * npages]
    nrow = 2 * H_C * nq
    q = q_ref[...] * (DK_C ** -0.5)
    row = lax.broadcasted_iota(jnp.int32, (nrow, 1), 0)
    lane = lax.broadcasted_iota(jnp.int32, (1, Q_C), 1)
    qs = jnp.concatenate([q] * (2 * H_C), axis=0)
    qs = jnp.where(row // nq == lane // DK_C, qs, 0.0).astype(BF16)
    s_pages = [_bdot(qs, kp[j][0, 0]) for j in range(npages)]
    s_new = lax.dot_general(qs, kn_ref[...].astype(BF16), _NT, preferred_element_type=F32)
    kidx = lax.broadcasted_iota(jnp.int32, (1, nq), 1)
    s_new = jnp.where(row % nq >= kidx, s_new, NEG_BIG)
    m = jnp.max(s_new, axis=-1, keepdims=True)
    for s in s_pages:
        m = jnp.maximum(m, jnp.max(s, axis=-1, keepdims=True))
    p_new = jnp.exp(s_new - m)
    l = jnp.sum(p_new, axis=-1, keepdims=True)
    p_pages = []
    for s in s_pages:
        p = jnp.exp(s - m)
        l = l + jnp.sum(p, axis=-1, keepdims=True)
        p_pages.append(p)
    coef = jnp.where((row // nq) % 2 == 0, 1.0, -lam_ref[0]) / l

    def diff(p):
        pw = p * coef
        return jnp.concatenate([pw[(2 * hh) * nq:(2 * hh + 1) * nq] + pw[(2 * hh + 1) * nq:(2 * hh + 2) * nq]
                                for hh in range(H_C)], axis=0)

    out = _bdot(diff(p_new), vn_ref[...])
    for j in range(npages):
        out = out + lax.dot_general(diff(p_pages[j]).astype(BF16), vp[j][0, 0].astype(BF16), _NT,
                                    preferred_element_type=F32)
    vlane = lax.broadcasted_iota(jnp.int32, (1, W_C), 1) // HEAD_DIM
    o = jnp.zeros((nq, W_C), F32)
    for hh in range(H_C):
        o = jnp.where(vlane == hh, out[hh * nq:(hh + 1) * nq], o)
    o = o * lax.rsqrt(_gsum(o * o, g64_ref[...]) * (1.0 / HEAD_DIM) + NORM_EPS) * g_ref[...] * out_scale
    o_ref[...] = o


def _attn_decode(page_table, lam, qn, kn, vn, g, cache_kt, cache_vt, layer, nq, out_scale):
    nb, npages = page_table.shape
    page = cache_kt.shape[3]
    row = pl.BlockSpec((nq, Q_C), lambda b, pt: (b, 0))

    def page_spec(j):
        return pl.BlockSpec((1, 1, Q_C, page), lambda b, pt, j=j: (layer, pt[b, j], 0, 0))

    grid_spec = pltpu.PrefetchScalarGridSpec(
        num_scalar_prefetch=1,
        grid=(nb,),
        in_specs=[pl.BlockSpec(memory_space=pltpu.SMEM), row, row, row,
                  pl.BlockSpec((1, W_C), lambda b, pt: (0, 0)), pl.BlockSpec((W_C, W_C), lambda b, pt: (0, 0))]
                 + [page_spec(j) for j in range(npages)] * 2,
        out_specs=row,
    )
    return pl.pallas_call(
        functools.partial(_attn_decode_body, npages=npages, nq=nq, out_scale=out_scale),
        grid_spec=grid_spec,
        out_shape=jax.ShapeDtypeStruct((nb * nq, W_C), F32),
        compiler_params=_cparams("parallel"),
        name="attn_decode",
    )(page_table, lam, qn, kn, vn, jnp.tile(g, H_C).reshape(1, W_C), _group_ones(W_C, HEAD_DIM),
      *([cache_kt] * npages), *([cache_vt] * npages))


def _rwkv_prep(x, prev, mu, w0, w2p, a0, a2p, g2, k_k, k_a, g64):
    xs = x + (prev - x) * mu
    r = xs[:, :W_D]
    k = xs[:, W_D:2 * W_D]
    v = xs[:, 2 * W_D:3 * W_D]
    hwa = xs[:, 3 * W_D:3 * W_D + LORA_W + LORA_A]
    hg = xs[:, 3 * W_D + LORA_W + LORA_A:]
    w_log = -jax.nn.softplus(-(w0 + _bdot(jnp.tanh(hwa), w2p))) - 0.5
    logdecay = -jnp.exp(w_log)
    a = jax.nn.sigmoid(a0 + _bdot(hwa, a2p))
    g = _bdot(jax.nn.sigmoid(hg), g2)
    kk = k * k_k
    kk = kk / jnp.maximum(jnp.sqrt(_gsum(kk * kk, g64)), 1e-12)
    k2 = k * (1.0 + (a - 1.0) * k_a)
    return r, logdecay, k2, v, kk, a, g


def _rwkv_post(o, r, k2, v, g, rk, lnw, lnb, g64):
    inv_n = 1.0 / HEAD_DIM
    mean = _gsum(o, g64) * inv_n
    d = o - mean
    var = _gsum(d * d, g64) * inv_n
    on = d * lax.rsqrt(var + RWKV_GN_EPS) * lnw + lnb
    bonus = _gsum(r * k2 * rk, g64) * v
    return (on + bonus) * g


def _wkv_chunk(r, lw, k, v, kk, a, state):
    c = r.shape[0]
    n = H_D * c
    row = lax.broadcasted_iota(jnp.int32, (c, 1), 0)
    cw = lw
    sh = 1
    while sh < c:
        cw = cw + jnp.where(row >= sh, pltpu.roll(cw, sh, 0), 0.0)
        sh *= 2
    tot = cw[c - 1:c]
    e_out = jnp.exp(-cw)
    e_fut = jnp.exp(tot - cw)
    beta = kk * a
    lane_head = lax.broadcasted_iota(jnp.int32, (1, W_D), 1) // HEAD_DIM

    def stack(y):
        return jnp.concatenate([jnp.where(lane_head == hh, y, 0.0) for hh in range(H_D)], axis=0).astype(BF16)

    lhs = jnp.concatenate([stack(-kk * jnp.exp(cw - lw)), stack(r * jnp.exp(cw))], axis=0)
    rhs = jnp.concatenate([stack(beta * e_out), stack(k * e_out)], axis=0)
    amat = lax.dot_general(lhs, rhs, _NT, preferred_element_type=F32)
    ri = lax.broadcasted_iota(jnp.int32, (n, n), 0)
    ci = lax.broadcasted_iota(jnp.int32, (n, n), 1)
    same = ri // c == ci // c
    strict = same & (ci < ri)
    incl = same & (ci <= ri)
    a_ab = jnp.where(strict, amat[:n, :n], 0.0)
    a_ak = jnp.where(strict, amat[:n, n:], 0.0).astype(BF16)
    a_r = jnp.concatenate([jnp.where(incl, amat[n:, :n], 0.0), jnp.where(incl, amat[n:, n:], 0.0)],
                          axis=1).astype(BF16)
    tinv = (ri == ci).astype(F32) + a_ab
    pw = a_ab.astype(BF16)
    m = 2
    while m < c:
        pw2 = jnp.dot(pw, pw, preferred_element_type=F32)
        tinv = tinv + _bdot(tinv, pw2)
        pw = pw2.astype(BF16)
        m *= 2
    xr = lax.dot_general(lhs, state.astype(BF16), _NT, preferred_element_type=F32)
    vs = stack(v)
    us = _bdot(tinv, xr[:n] + jnp.dot(a_ak, vs, preferred_element_type=F32)).astype(BF16)
    uv = jnp.concatenate([us, vs], axis=0)
    os_ = xr[n:] + jnp.dot(a_r, uv, preferred_element_type=F32)
    o = os_[0:c]
    for hh in range(1, H_D):
        o = o + os_[hh * c:(hh + 1) * c]
    upd = lax.dot_general(uv, jnp.concatenate([stack(beta * e_fut), stack(k * e_fut)], axis=0), _TN,
                          preferred_element_type=F32)
    return o, state * jnp.exp(tot) + upd


def _wkv_prompt_body(zd_ref, sb_ref, mu_ref, w0_ref, w2_ref, a0_ref, a2_ref, g2_ref, kk_ref, ka_ref, rk_ref,
                     lnw_ref, lnb_ref, g64_ref, yd_ref, shift_ref, state_ref,
                     st_s, prev_s, r_s, lw_s, k_s, v_s, kk_s, a_s, o_s, *, tt, nt):
    t = pl.program_id(1)

    @pl.when(t == 0)
    def _():
        st_s[...] = jnp.zeros_like(st_s)
        prev_s[...] = sb_ref[0]

    x = zd_ref[0]
    first = lax.broadcasted_iota(jnp.int32, (tt, 1), 0) == 0
    prev = jnp.where(first, prev_s[...], pltpu.roll(x, 1, 0))
    last = x[tt - 1:tt]
    prev_s[...] = last
    g64 = g64_ref[...]
    r, lw, k2, v, kk, a, g = _rwkv_prep(x, prev, mu_ref[...], w0_ref[...], w2_ref[...], a0_ref[...], a2_ref[...],
                                        g2_ref[...], kk_ref[...], ka_ref[...], g64)
    r_s[...] = r
    lw_s[...] = lw
    k_s[...] = k2
    v_s[...] = v
    kk_s[...] = kk
    a_s[...] = a

    def chunk_group(c, _):
        st = st_s[...]
        for u in range(WKV_GROUP):
            rows = pl.ds(pl.multiple_of((c * WKV_GROUP + u) * WKV_CHUNK, WKV_CHUNK), WKV_CHUNK)
            o, st = _wkv_chunk(r_s[rows, :], lw_s[rows, :], k_s[rows, :], v_s[rows, :], kk_s[rows, :], a_s[rows, :],
                               st)
            o_s[rows, :] = o
        st_s[...] = st
        return 0

    lax.fori_loop(0, tt // (WKV_CHUNK * WKV_GROUP), chunk_group, 0)
    yd_ref[0] = _rwkv_post(o_s[...], r, k2, v, g, rk_ref[...], lnw_ref[...], lnb_ref[...], g64)

    @pl.when(t == nt - 1)
    def _():
        shift_ref[0] = last
        state_ref[0] = st_s[...]


def _rwkv_params(mu, w0, w2, a0, a2, g2, k_k, k_a, r_k, lnw, lnb):
    zeros = jnp.zeros((LORA_W, W_D), F32)
    row = lambda x: x.reshape(1, -1)
    return (row(mu), row(w0), jnp.concatenate([w2, zeros], axis=0).astype(BF16), row(a0),
            jnp.concatenate([zeros, a2], axis=0).astype(BF16), g2.astype(BF16), row(k_k), row(k_a), row(r_k),
            row(lnw), row(lnb), _group_ones(W_D, HEAD_DIM))


_RWKV_PARAM_SHAPES = ((1, N_COLS_D), (1, W_D), (LORA_W + LORA_A, W_D), (1, W_D), (LORA_W + LORA_A, W_D),
                      (LORA_G, W_D), (1, W_D), (1, W_D), (1, W_D), (1, W_D), (1, W_D), (W_D, W_D))


def _wkv_prompt(zd3, shift_buf, params, tt):
    nb, L, _ = zd3.shape
    nt = L // tt
    sc = lambda: pltpu.VMEM((tt, W_D), F32)
    return pl.pallas_call(
        functools.partial(_wkv_prompt_body, tt=tt, nt=nt),
        grid=(nb, nt),
        in_specs=[pl.BlockSpec((1, tt, N_COLS_D), lambda b, t: (b, t, 0)),
                  pl.BlockSpec((1, 1, N_COLS_D), lambda b, t: (b, 0, 0))]
                 + [_full(s) for s in _RWKV_PARAM_SHAPES],
        out_specs=[pl.BlockSpec((1, tt, W_D), lambda b, t: (b, t, 0)),
                   pl.BlockSpec((1, 1, N_COLS_D), lambda b, t: (b, 0, 0)),
                   pl.BlockSpec((1, W_D, W_D), lambda b, t: (b, 0, 0))],
        out_shape=[jax.ShapeDtypeStruct((nb, L, W_D), F32), jax.ShapeDtypeStruct((nb, 1, N_COLS_D), F32),
                   jax.ShapeDtypeStruct((nb, W_D, W_D), F32)],
        scratch_shapes=[pltpu.VMEM((W_D, W_D), F32), pltpu.VMEM((1, N_COLS_D), F32)] + [sc() for _ in range(7)],
        compiler_params=_cparams("parallel", "arbitrary"),
        name="wkv_prompt",
    )(zd3, shift_buf, *params)


def _wkv_decode_body(zd_ref, sb_ref, s0_ref, mu_ref, w0_ref, w2_ref, a0_ref, a2_ref, g2_ref, kk_ref, ka_ref, rk_ref,
                     lnw_ref, lnb_ref, g64_ref, yd_ref, shift_ref, state_ref, *, bb, L):
    x3 = zd_ref[...]
    tpos = lax.broadcasted_iota(jnp.int32, (1, L, 1), 1)
    prev3 = jnp.where(tpos == 0, sb_ref[...], pltpu.roll(x3, 1, 1))
    shift_ref[...] = x3[:, L - 1:L, :]
    g64 = g64_ref[...]
    flat = lambda y: y.reshape(bb * L, y.shape[-1])
    r, lw, k2, v, kk, a, g = _rwkv_prep(flat(x3), flat(prev3), mu_ref[...], w0_ref[...], w2_ref[...], a0_ref[...],
                                        a2_ref[...], g2_ref[...], kk_ref[...], ka_ref[...], g64)
    cube = lambda y: y.reshape(bb, L, W_D)
    r3, w3, k3, v3, kk3, b3 = cube(r), cube(jnp.exp(lw)), cube(k2), cube(v), cube(kk), cube(kk * a)
    eye = (lax.broadcasted_iota(jnp.int32, (HEAD_DIM, HEAD_DIM), 0)
           == lax.broadcasted_iota(jnp.int32, (HEAD_DIM, HEAD_DIM), 1)).astype(F32)[None]
    outs = [[None] * H_D for _ in range(L)]
    for hh in range(H_D):
        sl = slice(hh * HEAD_DIM, (hh + 1) * HEAD_DIM)
        s = s0_ref[:, hh]
        for t in range(L):
            tok = lambda y: y[:, t:t + 1, sl]
            sa = jnp.sum(s * -tok(kk3), axis=-1, keepdims=True)
            vcol = jnp.sum(eye * tok(v3), axis=-1, keepdims=True)
            s = s * tok(w3) + sa * tok(b3) + vcol * tok(k3)
            ocol = jnp.sum(s * tok(r3), axis=-1, keepdims=True)
            outs[t][hh] = jnp.sum(eye * ocol, axis=1, keepdims=True)
        state_ref[:, hh] = s
    o3 = jnp.concatenate([jnp.concatenate(row, axis=-1) for row in outs], axis=1)
    y = _rwkv_post(flat(o3), r, k2, v, g, rk_ref[...], lnw_ref[...], lnb_ref[...], g64)
    yd_ref[...] = y.reshape(bb, L, W_D)


def _wkv_decode(zd3, shift_buf, s0, params, bb):
    nb, L, _ = zd3.shape
    blk = lambda *s: pl.BlockSpec((bb,) + s, lambda i: (i,) + (0,) * len(s))
    return pl.pallas_call(
        functools.partial(_wkv_decode_body, bb=bb, L=L),
        grid=(nb // bb,),
        in_specs=[blk(L, N_COLS_D), blk(1, N_COLS_D), blk(H_D, HEAD_DIM, HEAD_DIM)]
                 + [_full(s) for s in _RWKV_PARAM_SHAPES],
        out_specs=[blk(L, W_D), blk(1, N_COLS_D), blk(H_D, HEAD_DIM, HEAD_DIM)],
        out_shape=[jax.ShapeDtypeStruct((nb, L, W_D), F32), jax.ShapeDtypeStruct((nb, 1, N_COLS_D), F32),
                   jax.ShapeDtypeStruct((nb, H_D, HEAD_DIM, HEAD_DIM), F32)],
        compiler_params=_cparams("parallel"),
        name="wkv_decode",
    )(zd3, shift_buf, s0, *params)


def _outproj_body(h_ref, ya_ref, yb_ref, yc_ref, yd_ref, w_ref, o_ref, *, yc_transposed):
    acc = h_ref[...]
    acc = acc + _bdot(ya_ref[...], w_ref[0:W_A, :])
    acc = acc + _bdot(yb_ref[...], w_ref[W_A:W_A + W_B, :])
    off = W_A + W_B
    yc = yc_ref[...].T if yc_transposed else yc_ref[...]
    acc = acc + _bdot(yc, w_ref[off:off + W_C, :])
    acc = acc + _bdot(yd_ref[...], w_ref[off + W_C:, :])
    o_ref[...] = acc


def _outproj(h, ya, yb, yc, yd, w_bf, tm, yc_transposed):
    t = h.shape[0]
    row = lambda n: pl.BlockSpec((tm, n), lambda i: (i, 0))
    ycs = pl.BlockSpec((W_C, tm), lambda i: (0, i)) if yc_transposed else row(W_C)
    return pl.pallas_call(
        functools.partial(_outproj_body, yc_transposed=yc_transposed),
        grid=(t // tm,),
        in_specs=[row(D_MODEL), row(W_A), row(W_B), ycs, row(W_D), _full((D_MODEL, D_MODEL))],
        out_specs=row(D_MODEL),
        out_shape=jax.ShapeDtypeStruct((t, D_MODEL), F32),
        compiler_params=_cparams("parallel"),
        name="outproj",
    )(h, ya, yb, yc, yd, w_bf)


def _ffn_body(blk_ref, h_ref, g_ref, w1_ref, w3_ref, w2_ref, o_ref, hn_s, acc_s, *, residual):
    del blk_ref
    j = pl.program_id(1)

    @pl.when(j == 0)
    def _():
        x = h_ref[...]
        hn_s[...] = (x * lax.rsqrt(jnp.mean(x * x, axis=-1, keepdims=True) + NORM_EPS) * g_ref[...]).astype(BF16)
        acc_s[...] = x if residual else jnp.zeros_like(x)

    hn = hn_s[...]
    a = jnp.dot(hn, w1_ref[0], preferred_element_type=F32)
    b = jnp.dot(hn, w3_ref[0], preferred_element_type=F32)
    acc_s[...] += _bdot(jax.nn.silu(a) * b, w2_ref[0])

    @pl.when(j == pl.num_programs(1) - 1)
    def _():
        o_ref[...] = acc_s[...]


def _ffn(blk_e, x, g, w1, w3, w2, tm, tf, residual):
    t = x.shape[0]
    ff = w1.shape[2]
    grid_spec = pltpu.PrefetchScalarGridSpec(
        num_scalar_prefetch=1,
        grid=(t // tm, ff // tf),
        in_specs=[pl.BlockSpec((tm, D_MODEL), lambda i, j, e: (i, 0)),
                  pl.BlockSpec((1, D_MODEL), lambda i, j, e: (0, 0)),
                  pl.BlockSpec((1, D_MODEL, tf), lambda i, j, e: (e[i], 0, j)),
                  pl.BlockSpec((1, D_MODEL, tf), lambda i, j, e: (e[i], 0, j)),
                  pl.BlockSpec((1, tf, D_MODEL), lambda i, j, e: (e[i], j, 0))],
        out_specs=pl.BlockSpec((tm, D_MODEL), lambda i, j, e: (i, 0)),
        scratch_shapes=[pltpu.VMEM((tm, D_MODEL), BF16), pltpu.VMEM((tm, D_MODEL), F32)],
    )
    return pl.pallas_call(
        functools.partial(_ffn_body, residual=residual),
        grid_spec=grid_spec,
        out_shape=jax.ShapeDtypeStruct((t, D_MODEL), F32),
        compiler_params=_cparams("parallel", "arbitrary"),
        name="ffn" if residual else "moe_ffn",
    )(blk_e, x, g.reshape(1, D_MODEL), w1, w3, w2)


def _router_body(h_ref, g_ref, w_ref, lg_ref):
    x = h_ref[...]
    hn = x * lax.rsqrt(jnp.mean(x * x, axis=-1, keepdims=True) + NORM_EPS) * g_ref[...]
    lg_ref[...] = _hdot(hn, w_ref[...])


def _router(h, g, w_pad, tm):
    t = h.shape[0]
    row = lambda n: pl.BlockSpec((tm, n), lambda i: (i, 0))
    return pl.pallas_call(
        _router_body,
        grid=(t // tm,),
        in_specs=[row(D_MODEL), _full((1, D_MODEL)), _full((D_MODEL, 128))],
        out_specs=row(128),
        out_shape=jax.ShapeDtypeStruct((t, 128), F32),
        compiler_params=_cparams("parallel"),
        name="router",
    )(h, g.reshape(1, D_MODEL), w_pad)


def _moe(h, g, router, w1, w3, w2, tm, tf):
    t = h.shape[0]
    logits = _router(h, g, jnp.pad(router, ((0, 0), (0, 128 - N_EXPERTS))), tm)
    top_logit, top_e = lax.top_k(logits[:, :N_EXPERTS], TOP_K)
    gates = jax.nn.softmax(top_logit, axis=-1)
    tk = t * TOP_K
    flat_e = top_e.reshape(tk)
    onehot = (flat_e[:, None] == jnp.arange(N_EXPERTS)[None, :]).astype(jnp.int32)
    rank = jnp.take_along_axis(jnp.cumsum(onehot, axis=0) - onehot, flat_e[:, None], axis=1)[:, 0]
    nblk = (jnp.sum(onehot, axis=0) + tm - 1) // tm
    blk_end = jnp.cumsum(nblk)
    dest = (blk_end - nblk)[flat_e] * tm + rank
    nb = -(-tk // tm) + N_EXPERTS
    row_tok = jnp.full((nb * tm,), t, jnp.int32).at[dest].set(jnp.arange(tk, dtype=jnp.int32) // TOP_K)
    blk_e = jnp.minimum(jnp.searchsorted(blk_end, jnp.arange(nb), side='right'), N_EXPERTS - 1).astype(jnp.int32)
    xg = jnp.concatenate([h, jnp.zeros((1, D_MODEL), F32)], axis=0)[row_tok]
    yb = _ffn(blk_e, xg, g, w1, w3, w2, tm, tf, False)
    d2 = dest.reshape(t, TOP_K)
    return h + (yb[d2[:, 0]] * gates[:, 0:1] + yb[d2[:, 1]] * gates[:, 1:2])


def _ple_body(h_ref, g_ref, wg_ref, p_ref, wp_ref, o_ref):
    x = h_ref[...]
    hn = (x * lax.rsqrt(jnp.mean(x * x, axis=-1, keepdims=True) + NORM_EPS) * g_ref[...]).astype(BF16)
    gate = jax.nn.sigmoid(jnp.dot(hn, wg_ref[...], preferred_element_type=F32))
    o_ref[...] = x + gate * _bdot(p_ref[...], wp_ref[...])


def _ple(h, g, wg_bf, p, wp_bf, tm):
    t = h.shape[0]
    row = lambda n: pl.BlockSpec((tm, n), lambda i: (i, 0))
    return pl.pallas_call(
        _ple_body,
        grid=(t // tm,),
        in_specs=[row(D_MODEL), _full((1, D_MODEL)), _full((D_MODEL, D_MODEL)), row(PLE_DIM),
                  _full((PLE_DIM, D_MODEL))],
        out_specs=row(D_MODEL),
        out_shape=jax.ShapeDtypeStruct((t, D_MODEL), F32),
        compiler_params=_cparams("parallel"),
        name="ple",
    )(h, g.reshape(1, D_MODEL), wg_bf, p, wp_bf)


def _row_tile(t, want):
    tm = min(t, want)
    assert t % tm == 0, (t, tm)
    return tm


def _run(x, p, decode, wts, cache, state_pool, state_shift, state_wkv, page_table):
    nb, L, _ = x.shape
    t = nb * L
    tm = _row_tile(t, 512)
    n_past = page_table.shape[1] * cache[0].shape[3] if decode else 0
    h = x.reshape(t, D_MODEL)
    ks, vs, pools, shifts, wkvs, chunk_vs = [], [], [], [], [], []
    causal = jnp.tril(jnp.ones((CHUNK, CHUNK), bool))
    for l in range(DEPTH):
        w = wts[l]
        za, zb, qn, kn, v, zd, qh, kh, vh = _inproj(h, w["norm_mix_g"], w["w_in"], w["c_qnorm_g"], w["c_knorm_g"], tm)

        ws = jnp.where(causal, w["a_ws"], 0.0)
        bs = w["a_bs"]
        if decode:
            reps = CHUNK // L
            ws = jax.vmap(lambda m: jnp.kron(jnp.eye(reps, dtype=F32), m[:L, :L]))(ws)
            bs = jnp.tile(bs[:, :L], (1, reps))
        wcat = jnp.concatenate([ws[hh] for hh in range(H_A)], axis=1).astype(BF16)
        bias = jnp.repeat(bs.T, HEAD_DIM, axis=1)
        ya, va = _mix_a(za, w["a_vnorm_g"], wcat, bias, _row_tile(t, 512) // CHUNK)

        if decode:
            buf16 = jnp.pad(state_pool[l], ((0, 0), (POOL_HALO - POOL_BUF, 0), (0, 0)))
        else:
            buf16 = jnp.zeros((nb, POOL_HALO, W_B), F32)
        proj_bd = jax.scipy.linalg.block_diag(*[w["b_proj"][gi] for gi in range(len(POOL_WINDOWS))]).astype(BF16)
        yb, pool16 = _mix_b(zb.reshape(nb, L, W_B), buf16, proj_bd, w["b_scale"], _row_tile(L, 512), n_past)
        yb = yb.reshape(t, W_B)

        lam_init = 0.8 - 0.6 * math.exp(-0.3 * l)
        lp = w["c_lambda"]
        lam = (jnp.exp(jnp.sum(lp[0] * lp[1])) - jnp.exp(jnp.sum(lp[2] * lp[3])) + lam_init).reshape(1)
        if decode:
            yc = _attn_decode(page_table, lam, qn, kn, v, w["c_subln_g"], cache[0], cache[1], l, L, 1.0 - lam_init)
        else:
            tq = _row_tile(L, ATTN_TQ)
            yc = _attn_prompt(lam, qh, kh, vh, w["c_subln_g"], nb, L, tq, _row_tile(L, max(tq, ATTN_TK)), ATTN_NSUB,
                              1.0 - lam_init)

        params = _rwkv_params(w["d_mu"], w["d_w0"], w["d_w2"], w["d_a0"], w["d_a2"], w["d_g2"], w["d_kk"], w["d_ka"],
                              w["d_rk"].reshape(-1), w["d_lnx_w"], w["d_lnx_b"])
        zd3 = zd.reshape(nb, L, N_COLS_D)
        if decode:
            yd, shift_new, wkv_new = _wkv_decode(zd3, state_shift[l], state_wkv[l], params, 8)
        else:
            yd, shift_new, st = _wkv_prompt(zd3, jnp.zeros((nb, 1, N_COLS_D), F32), params, _row_tile(L, 512))
            st = st.reshape(nb, H_D, HEAD_DIM, H_D, HEAD_DIM)
            wkv_new = jnp.stack([st[:, hh, :, hh, :] for hh in range(H_D)], axis=1)
        h = _outproj(h, ya, yb, yc, yd.reshape(t, W_D), w["w_out"], tm, yc_transposed=not decode)

        if l % 2 == 0:
            h = _ffn(jnp.zeros((t // tm,), jnp.int32), h, w["norm_ffn_g"], w["ffn_w1"], w["ffn_w3"], w["ffn_w2"],
                     tm, D_FF // 2, True)
        else:
            h = _moe(h, w["norm_ffn_g"], w["moe_router"], w["moe_w1"], w["moe_w3"], w["moe_w2"], tm, D_FF // 2)

        h = _ple(h, w["ple_norm_g"], w["ple_gate"], p[l].reshape(t, PLE_DIM), w["ple_proj"], tm)

        ks.append(kn.reshape(nb, L, H_C, 2, DK_C))
        vs.append(v.reshape(nb, L, H_C, HEAD_DIM))
        pools.append(pool16[:, POOL_HALO - POOL_BUF:])
        shifts.append(shift_new)
        wkvs.append(wkv_new)
        chunk_vs.append(va.reshape(nb, L, W_A))
    return (h.reshape(nb, L, D_MODEL), jnp.stack(ks), jnp.stack(vs), jnp.stack(pools), jnp.stack(shifts),
            jnp.stack(wkvs), jnp.stack(chunk_vs))


def kernel(x_prompt, x_sample, cache_k, cache_v, state_pool, state_shift, state_wkv, page_table, p_prompt, p_sample, norm_mix_g, w_in, w_out, a_vnorm_g, a_ws, a_bs, b_proj, b_scale, c_qnorm_g, c_knorm_g, c_lambda, c_subln_g, d_mu, d_w0, d_w2, d_a0, d_a2, d_g2, d_kk, d_ka, d_rk, d_lnx_w, d_lnx_b, norm_ffn_g, ffn_w1, ffn_w3, ffn_w2, moe_router, moe_w1, moe_w3, moe_w2, ple_norm_g, ple_gate, ple_proj):
    bf = lambda a: a.astype(BF16)
    wts = []
    for l in range(DEPTH):
        w = dict(norm_mix_g=norm_mix_g[l], w_in=bf(w_in[l]), w_out=bf(w_out[l]), a_vnorm_g=a_vnorm_g[l],
                 a_ws=a_ws[l], a_bs=a_bs[l], b_proj=b_proj[l], b_scale=b_scale[l], c_qnorm_g=c_qnorm_g[l],
                 c_knorm_g=c_knorm_g[l], c_lambda=c_lambda[l], c_subln_g=c_subln_g[l], d_mu=d_mu[l], d_w0=d_w0[l],
                 d_w2=d_w2[l], d_a0=d_a0[l], d_a2=d_a2[l], d_g2=d_g2[l], d_kk=d_kk[l], d_ka=d_ka[l], d_rk=d_rk[l],
                 d_lnx_w=d_lnx_w[l], d_lnx_b=d_lnx_b[l], norm_ffn_g=norm_ffn_g[l], ple_norm_g=ple_norm_g[l],
                 ple_gate=bf(ple_gate[l]), ple_proj=bf(ple_proj[l]))
        if l % 2 == 0:
            w.update(ffn_w1=bf(ffn_w1[l // 2])[None], ffn_w3=bf(ffn_w3[l // 2])[None], ffn_w2=bf(ffn_w2[l // 2])[None])
        else:
            w.update(moe_router=moe_router[l // 2], moe_w1=bf(moe_w1[l // 2]), moe_w3=bf(moe_w3[l // 2]),
                     moe_w2=bf(moe_w2[l // 2]))
        wts.append(w)
    depth, n_phys, page = cache_k.shape[:3]
    cache = (jnp.transpose(cache_k, (0, 1, 3, 4, 5, 2)).reshape(depth, n_phys, Q_C, page),
             jnp.transpose(cache_v, (0, 1, 3, 4, 2)).reshape(depth, n_phys, W_C, page))
    y_p, k_p, v_p, pool_p, shift_p, wkv_p, _ = _run(x_prompt, p_prompt, False, wts, cache, state_pool, state_shift,
                                                    state_wkv, page_table)
    y_s, k_s, v_s, pool_s, shift_s, wkv_s, cv_s = _run(x_sample, p_sample, True, wts, cache, state_pool, state_shift,
                                                       state_wkv, page_table)
    return (y_p, y_s, k_p, v_p, pool_p, shift_p, wkv_p, k_s, v_s, pool_s, shift_s, wkv_s, cv_s)
```

```python
import functools
import math

import jax
import jax.numpy as jnp
from jax import lax
from jax.experimental import pallas as pl
from jax.experimental.pallas import tpu as pltpu

F32 = jnp.float32
BF16 = jnp.bfloat16

D_MODEL = 1024
DEPTH = 4
HEAD_DIM = 64
W_A = W_B = W_C = W_D = 256
H_A = H_C = H_D = 4
CHUNK = 128
POOL_WINDOWS = (2, 4, 8, 16)
POOL_BUF = 15
POOL_HALO = 16
DK_C = 32
Q_C = 256
LORA_W = 64
LORA_A = 64
LORA_G = 128
N_COLS_D = 3 * W_D + LORA_W + LORA_A + LORA_G
OFF_A = 0
OFF_B = 512
OFF_C = 768
OFF_D = 1536
N_IN = 2560
D_FF = 2816
N_EXPERTS = 8
TOP_K = 2
PLE_DIM = 256
NORM_EPS = 1e-6
RWKV_GN_EPS = 64e-5
NEG_BIG = -1e30
LOG2_E = 1.4426950408889634

ATTN_TQ, ATTN_TK, ATTN_NSUB = 1024, 1024, 2
WKV_GROUP = 8
WKV_CHUNK = 64
VMEM_LIMIT = 48 * 1024 * 1024

_NT = (((1,), (1,)), ((), ()))
_TN = (((0,), (0,)), ((), ()))


def _cparams(*sem):
    return pltpu.CompilerParams(dimension_semantics=sem, vmem_limit_bytes=VMEM_LIMIT)


def _bdot(a, b):
    return jnp.dot(a.astype(BF16), b.astype(BF16), preferred_element_type=F32)


def _hdot(a, b, dims=None):
    if dims is None:
        return jnp.dot(a, b, preferred_element_type=F32, precision=lax.Precision.HIGHEST)
    return lax.dot_general(a, b, dims, preferred_element_type=F32, precision=lax.Precision.HIGHEST)


def _gsum(x, gmat):
    hi = x.astype(BF16)
    lo = (x - hi.astype(F32)).astype(BF16)
    return (jnp.dot(hi, gmat, preferred_element_type=F32) + jnp.dot(lo, gmat, preferred_element_type=F32))


def _group_ones(n, g):
    i = jnp.arange(n) // g
    return (i[:, None] == i[None, :]).astype(BF16)


def _full(shape):
    nd = len(shape)
    return pl.BlockSpec(shape, lambda *_: (0,) * nd)


def _inproj_body(h_ref, g_ref, w_ref, gq_ref, gk_ref, g32_ref,
                 za_ref, zb_ref, qn_ref, kn_ref, v_ref, zd_ref, qh_ref, kh_ref, vt_ref):
    x = h_ref[...]
    xn = (x * lax.rsqrt(jnp.mean(x * x, axis=-1, keepdims=True) + NORM_EPS) * g_ref[...]).astype(BF16)

    def proj(lo, hi):
        return jnp.dot(xn, w_ref[:, lo:hi], preferred_element_type=F32)

    za_ref[...] = proj(OFF_A, OFF_B)
    zb_ref[...] = proj(OFF_B, OFF_C)
    zd_ref[...] = proj(OFF_D, N_IN)
    zq = proj(OFF_C, OFF_C + Q_C)
    zk = proj(OFF_C + Q_C, OFF_C + 2 * Q_C)
    zv = proj(OFF_C + 2 * Q_C, OFF_D)
    g32 = g32_ref[...]
    qn = zq * lax.rsqrt(_gsum(zq * zq, g32) * (1.0 / DK_C) + NORM_EPS) * gq_ref[...]
    kn = zk * lax.rsqrt(_gsum(zk * zk, g32) * (1.0 / DK_C) + NORM_EPS) * gk_ref[...]
    qn_ref[...] = qn
    kn_ref[...] = kn
    v_ref[...] = zv
    qs = qn * (DK_C ** -0.5 * LOG2_E)
    for hh in range(H_C):
        sl = slice(hh * HEAD_DIM, (hh + 1) * HEAD_DIM)
        qh_ref[hh] = qs[:, sl].astype(BF16)
        kh_ref[hh] = kn[:, sl].astype(BF16)
    vt_ref[...] = zv.T.astype(BF16)


def _inproj(h, g, w_bf, gq, gk, tm):
    t = h.shape[0]
    row = lambda n: pl.BlockSpec((tm, n), lambda i: (i, 0))
    headm = pl.BlockSpec((H_C, tm, HEAD_DIM), lambda i: (0, i, 0))
    f = lambda n: jax.ShapeDtypeStruct((t, n), F32)
    hm = jax.ShapeDtypeStruct((H_C, t, HEAD_DIM), BF16)
    return pl.pallas_call(
        _inproj_body,
        grid=(t // tm,),
        in_specs=[row(D_MODEL), _full((1, D_MODEL)), _full((D_MODEL, N_IN)), _full((1, Q_C)), _full((1, Q_C)),
                  _full((Q_C, Q_C))],
        out_specs=[row(2 * W_A), row(W_B), row(Q_C), row(Q_C), row(W_C), row(N_COLS_D), headm, headm,
                   pl.BlockSpec((W_C, tm), lambda i: (0, i))],
        out_shape=[f(2 * W_A), f(W_B), f(Q_C), f(Q_C), f(W_C), f(N_COLS_D), hm, hm,
                   jax.ShapeDtypeStruct((W_C, t), BF16)],
        compiler_params=_cparams("parallel"),
        name="inproj",
    )(h, g.reshape(1, D_MODEL), w_bf, jnp.tile(gq, 2 * H_C).reshape(1, Q_C), jnp.tile(gk, 2 * H_C).reshape(1, Q_C),
      _group_ones(Q_C, DK_C))


def _mix_a_body(z_ref, vg_ref, g64_ref, wcat_ref, bias_ref, ya_ref, va_ref, *, nchunk):
    z = z_ref[...]
    u = jax.nn.gelu(z[:, :W_A])
    vv = jax.nn.gelu(z[:, W_A:])
    vn = vv * lax.rsqrt(_gsum(vv * vv, g64_ref[...]) * (1.0 / HEAD_DIM) + NORM_EPS) * vg_ref[...]
    va_ref[...] = vn
    head = lax.broadcasted_iota(jnp.int32, (1, W_A), 1) // HEAD_DIM
    wcat = wcat_ref[...]
    bias = bias_ref[...]
    for c in range(nchunk):
        rows = slice(c * CHUNK, (c + 1) * CHUNK)
        vc = vn[rows]
        vstack = jnp.concatenate([jnp.where(head == hh, vc, 0.0) for hh in range(H_A)], axis=0).astype(BF16)
        s = jnp.dot(wcat, vstack, preferred_element_type=F32) + bias
        ya_ref[rows, :] = u[rows] * s


def _mix_a(za, vg, wcat_bf, bias, nchunk):
    t = za.shape[0]
    tm = nchunk * CHUNK
    return pl.pallas_call(
        functools.partial(_mix_a_body, nchunk=nchunk),
        grid=(t // tm,),
        in_specs=[pl.BlockSpec((tm, 2 * W_A), lambda i: (i, 0)), _full((1, W_A)), _full((W_A, W_A)),
                  _full((CHUNK, H_A * CHUNK)), _full((CHUNK, W_A))],
        out_specs=[pl.BlockSpec((tm, W_A), lambda i: (i, 0)), pl.BlockSpec((tm, W_A), lambda i: (i, 0))],
        out_shape=[jax.ShapeDtypeStruct((t, W_A), F32), jax.ShapeDtypeStruct((t, W_A), F32)],
        compiler_params=_cparams("parallel"),
        name="mix_a",
    )(za, vg.reshape(1, W_A), _group_ones(W_A, HEAD_DIM), wcat_bf, bias)


def _mix_b_body(z_ref, buf_ref, proj_ref, scale_ref, y_ref, pool_ref, halo_ref, *, bb, tl, start_pos, nt):
    t = pl.program_id(1)

    @pl.when(t == 0)
    def _():
        halo_ref[...] = buf_ref[...]

    z = z_ref[...]
    x = jnp.concatenate([halo_ref[...], z], axis=1)
    a2 = x + pltpu.roll(x, 1, 1)
    a4 = a2 + pltpu.roll(a2, 2, 1)
    a8 = a4 + pltpu.roll(a4, 4, 1)
    a16 = a8 + pltpu.roll(a8, 8, 1)
    grp = lax.broadcasted_iota(jnp.int32, (1, 1, W_B), 2) // (W_B // len(POOL_WINDOWS))
    win = jnp.where(grp == 0, a2, jnp.where(grp == 1, a4, jnp.where(grp == 2, a8, a16)))[:, POOL_HALO:]
    wsize = jnp.where(grp == 0, 2, jnp.where(grp == 1, 4, jnp.where(grp == 2, 8, 16)))
    pos = start_pos + t * tl + lax.broadcasted_iota(jnp.int32, (1, tl, 1), 1)
    cnt = jnp.minimum(wsize, pos + 1).astype(F32)
    pooled = (win / cnt - z).reshape(bb * tl, W_B)
    y_ref[...] = (_bdot(pooled, proj_ref[...]) * scale_ref[...]).reshape(bb, tl, W_B)
    tail = x[:, tl:]
    halo_ref[...] = tail

    @pl.when(t == nt - 1)
    def _():
        pool_ref[...] = tail


def _mix_b(zb3, buf16, proj_bd_bf, scale, bb, tl, start_pos):
    nb, L, _ = zb3.shape
    nt = L // tl
    return pl.pallas_call(
        functools.partial(_mix_b_body, bb=bb, tl=tl, start_pos=start_pos, nt=nt),
        grid=(nb // bb, nt),
        in_specs=[pl.BlockSpec((bb, tl, W_B), lambda b, t: (b, t, 0)),
                  pl.BlockSpec((bb, POOL_HALO, W_B), lambda b, t: (b, 0, 0)),
                  _full((W_B, W_B)), _full((1, W_B))],
        out_specs=[pl.BlockSpec((bb, tl, W_B), lambda b, t: (b, t, 0)),
                   pl.BlockSpec((bb, POOL_HALO, W_B), lambda b, t: (b, 0, 0))],
        out_shape=[jax.ShapeDtypeStruct((nb, L, W_B), F32), jax.ShapeDtypeStruct((nb, POOL_HALO, W_B), F32)],
        scratch_shapes=[pltpu.VMEM((bb, POOL_HALO, W_B), F32)],
        compiler_params=_cparams("parallel", "arbitrary"),
        name="mix_b",
    )(zb3, buf16, proj_bd_bf, scale.reshape(1, W_B))


def _attn_body(lam_ref, q_ref, k_ref, vt_ref, g_ref, o_ref, *, tq, tk, nsub, out_scale):
    qi = pl.program_id(2)
    q = q_ref[0]
    lane = lax.broadcasted_iota(jnp.int32, (1, HEAD_DIM), 1)
    zero = jnp.zeros_like(q)
    q0 = jnp.where(lane < DK_C, q, zero)
    q1 = jnp.where(lane >= DK_C, q, zero)

    def soft(s, vt, m, l, acc):
        mn = jnp.maximum(m, jnp.max(s, axis=0, keepdims=True))
        p = jnp.exp2(s - mn)
        al = jnp.exp2(m - mn)
        return mn, al * l + jnp.sum(p, axis=0, keepdims=True), al * acc + jnp.dot(
            vt, p.astype(BF16), preferred_element_type=F32)

    ts = tk // nsub

    def block(j, carry, masked):
        m0, l0, a0, m1, l1, a1 = carry
        for u in range(nsub):
            off = pl.multiple_of(j * tk + u * ts, ts)
            kb = k_ref[0, pl.ds(off, ts), :]
            vt = vt_ref[:, pl.ds(off, ts)]
            s0 = lax.dot_general(kb, q0, _NT, preferred_element_type=F32)
            s1 = lax.dot_general(kb, q1, _NT, preferred_element_type=F32)
            if masked:
                key = off + lax.broadcasted_iota(jnp.int32, (ts, tq), 0)
                qry = qi * tq + lax.broadcasted_iota(jnp.int32, (ts, tq), 1)
                s0 = jnp.where(key <= qry, s0, NEG_BIG)
                s1 = jnp.where(key <= qry, s1, NEG_BIG)
            m0, l0, a0 = soft(s0, vt, m0, l0, a0)
            m1, l1, a1 = soft(s1, vt, m1, l1, a1)
        return m0, l0, a0, m1, l1, a1

    rowv = lambda v: jnp.full((1, tq), v, F32)
    acc = jnp.zeros((HEAD_DIM, tq), F32)
    carry = (rowv(NEG_BIG), rowv(0.0), acc, rowv(NEG_BIG), rowv(0.0), acc)
    nfull = (qi * tq) // tk
    carry = lax.fori_loop(0, nfull, functools.partial(block, masked=False), carry)
    m0, l0, a0, m1, l1, a1 = block(nfull, carry, True)
    o = a0 / l0 - lam_ref[0] * (a1 / l1)
    o_ref[...] = o * lax.rsqrt(jnp.mean(o * o, axis=0, keepdims=True) + NORM_EPS) * g_ref[...] * out_scale


def _attn_prompt(lam, qh, kh, vt, g, nb, seq, tq, tk, nsub, out_scale):
    nq = seq // tq
    assert tk % tq == 0 and seq % tk == 0 and tk % nsub == 0, (seq, tq, tk, nsub)
    qspec = pl.BlockSpec((1, tq, HEAD_DIM), lambda b, h, i: (h, b * nq + i, 0))
    kspec = pl.BlockSpec((1, seq, HEAD_DIM), lambda b, h, i: (h, b, 0))
    vspec = pl.BlockSpec((HEAD_DIM, seq), lambda b, h, i: (h, b))
    return pl.pallas_call(
        functools.partial(_attn_body, tq=tq, tk=tk, nsub=nsub, out_scale=out_scale),
        grid=(nb, H_C, nq),
        in_specs=[pl.BlockSpec(memory_space=pltpu.SMEM), qspec, kspec, vspec, _full((HEAD_DIM, 1))],
        out_specs=pl.BlockSpec((HEAD_DIM, tq), lambda b, h, i: (h, b * nq + i)),
        out_shape=jax.ShapeDtypeStruct((W_C, nb * seq), F32),
        compiler_params=_cparams("parallel", "parallel", "parallel"),
        name="attn_prompt",
    )(lam, qh, kh, vt, g.reshape(HEAD_DIM, 1))


def _attn_decode_body(pt_ref, lam_ref, q_ref, kn_ref, vn_ref, g_ref, g64_ref, *rest, npages, nq, out_scale):
    del pt_ref
    kp = rest[:npages]
    vp = rest[npages:2 * npages]
    o_ref = rest[2 * npages]
    nrow = 2 * H_C * nq
    q = q_ref[...] * (DK_C ** -0.5)
    row = lax.broadcasted_iota(jnp.int32, (nrow, 1), 0)
    lane = lax.broadcasted_iota(jnp.int32, (1, Q_C), 1)
    qs = jnp.concatenate([q] * (2 * H_C), axis=0)
    qs = jnp.where(row // nq == lane // DK_C, qs, 0.0).astype(BF16)
    s_pages = [_bdot(qs, kp[j][0, 0]) for j in range(npages)]
    s_new = lax.dot_general(qs, kn_ref[...].astype(BF16), _NT, preferred_element_type=F32)
    kidx = lax.broadcasted_iota(jnp.int32, (1, nq), 1)
    s_new = jnp.where(row % nq >= kidx, s_new, NEG_BIG)
    m = jnp.max(s_new, axis=-1, keepdims=True)
    for s in s_pages:
        m = jnp.maximum(m, jnp.max(s, axis=-1, keepdims=True))
    p_new = jnp.exp(s_new - m)
    l = jnp.sum(p_new, axis=-1, keepdims=True)
    p_pages = []
    for s in s_pages:
        p = jnp.exp(s - m)
        l = l + jnp.sum(p, axis=-1, keepdims=True)
        p_pages.append(p)
    coef = jnp.where((row // nq) % 2 == 0, 1.0, -lam_ref[0]) / l

    def diff(p):
        pw = p * coef
        return jnp.concatenate([pw[(2 * hh) * nq:(2 * hh + 1) * nq] + pw[(2 * hh + 1) * nq:(2 * hh + 2) * nq]
                                for hh in range(H_C)], axis=0)

    out = _bdot(diff(p_new), vn_ref[...])
    for j in range(npages):
        out = out + lax.dot_general(diff(p_pages[j]).astype(BF16), vp[j][0, 0].astype(BF16), _NT,
                                    preferred_element_type=F32)
    vlane = lax.broadcasted_iota(jnp.int32, (1, W_C), 1) // HEAD_DIM
    o = jnp.zeros((nq, W_C), F32)
    for hh in range(H_C):
        o = jnp.where(vlane == hh, out[hh * nq:(hh + 1) * nq], o)
    o = o * lax.rsqrt(_gsum(o * o, g64_ref[...]) * (1.0 / HEAD_DIM) + NORM_EPS) * g_ref[...] * out_scale
    o_ref[...] = o


def _attn_decode(page_table, lam, qn, kn, vn, g, cache_kt, cache_vt, layer, nq, out_scale):
    nb, npages = page_table.shape
    page = cache_kt.shape[3]
    row = pl.BlockSpec((nq, Q_C), lambda b, pt: (b, 0))

    def page_spec(j):
        return pl.BlockSpec((1, 1, Q_C, page), lambda b, pt, j=j: (layer, pt[b, j], 0, 0))

    grid_spec = pltpu.PrefetchScalarGridSpec(
        num_scalar_prefetch=1,
        grid=(nb,),
        in_specs=[pl.BlockSpec(memory_space=pltpu.SMEM), row, row, row,
                  pl.BlockSpec((1, W_C), lambda b, pt: (0, 0)), pl.BlockSpec((W_C, W_C), lambda b, pt: (0, 0))]
                 + [page_spec(j) for j in range(npages)] * 2,
        out_specs=row,
    )
    return pl.pallas_call(
        functools.partial(_attn_decode_body, npages=npages, nq=nq, out_scale=out_scale),
        grid_spec=grid_spec,
        out_shape=jax.ShapeDtypeStruct((nb * nq, W_C), F32),
        compiler_params=_cparams("parallel"),
        name="attn_decode",
    )(page_table, lam, qn, kn, vn, jnp.tile(g, H_C).reshape(1, W_C), _group_ones(W_C, HEAD_DIM),
      *([cache_kt] * npages), *([cache_vt] * npages))


def _rwkv_prep(x, prev, mu, w0, w2p, a0, a2p, g2, k_k, k_a, g64):
    xs = x + (prev - x) * mu
    r = xs[:, :W_D]
    k = xs[:, W_D:2 * W_D]
    v = xs[:, 2 * W_D:3 * W_D]
    hwa = xs[:, 3 * W_D:3 * W_D + LORA_W + LORA_A]
    hg = xs[:, 3 * W_D + LORA_W + LORA_A:]
    w_log = -jax.nn.softplus(-(w0 + _bdot(jnp.tanh(hwa), w2p))) - 0.5
    logdecay = -jnp.exp(w_log)
    a = jax.nn.sigmoid(a0 + _bdot(hwa, a2p))
    g = _bdot(jax.nn.sigmoid(hg), g2)
    kk = k * k_k
    kk = kk / jnp.maximum(jnp.sqrt(_gsum(kk * kk, g64)), 1e-12)
    k2 = k * (1.0 + (a - 1.0) * k_a)
    return r, logdecay, k2, v, kk, a, g


def _rwkv_post(o, r, k2, v, g, rk, lnw, lnb, g64):
    inv_n = 1.0 / HEAD_DIM
    mean = _gsum(o, g64) * inv_n
    d = o - mean
    var = _gsum(d * d, g64) * inv_n
    on = d * lax.rsqrt(var + RWKV_GN_EPS) * lnw + lnb
    bonus = _gsum(r * k2 * rk, g64) * v
    return (on + bonus) * g


def _wkv_chunks(chunks, state):
    c = chunks[0][0].shape[0]
    n = H_D * c
    row = lax.broadcasted_iota(jnp.int32, (c, 1), 0)
    lane_head = lax.broadcasted_iota(jnp.int32, (1, W_D), 1) // HEAD_DIM
    row_head = lax.broadcasted_iota(jnp.int32, (W_D, 1), 0) // HEAD_DIM
    row_blk = lax.broadcasted_iota(jnp.int32, (n, 1), 0) // c
    col_blk = lax.broadcasted_iota(jnp.int32, (1, n), 1) // c
    ti = lax.broadcasted_iota(jnp.int32, (c, n), 0)
    si = lax.broadcasted_iota(jnp.int32, (c, n), 1) % c
    strict = si < ti
    incl = si <= ti

    def stack(y):
        return jnp.concatenate([jnp.where(lane_head == hh, y, 0.0) for hh in range(H_D)], axis=0).astype(BF16)

    def blockdiag(p):
        return jnp.where(row_blk == col_blk, jnp.concatenate([p] * H_D, axis=0), 0.0).astype(BF16)

    pre = []
    for r, lw, k, v, kk, a in chunks:
        cw = lw
        sh = 1
        while sh < c:
            cw = cw + jnp.where(row >= sh, pltpu.roll(cw, sh, 0), 0.0)
            sh *= 2
        tot = cw[c - 1:c]
        e_out = jnp.exp(-cw)
        e_fut = jnp.exp(tot - cw)
        beta = kk * a
        lhs = jnp.concatenate([-kk * jnp.exp(cw - lw), r * jnp.exp(cw)], axis=0).astype(BF16)
        rhs = jnp.concatenate([stack(beta * e_out), stack(k * e_out)], axis=0)
        fut = jnp.concatenate([beta * e_fut, k * e_fut], axis=0).astype(BF16)
        pre.append((lhs, rhs, fut, jnp.exp(tot), stack(v), v))
    amats = [lax.dot_general(lhs, rhs, _NT, preferred_element_type=F32) for lhs, rhs, *_ in pre]
    a_ab = [jnp.where(strict, am[:c, :n], 0.0) for am in amats]
    a_ak = [jnp.where(strict, am[:c, n:], 0.0) for am in amats]
    a_r = [jnp.concatenate([jnp.where(incl, am[c:, :n], 0.0), jnp.where(incl, am[c:, n:], 0.0)], axis=1)
           for am in amats]
    tinv = [(si == ti).astype(F32) + x for x in a_ab]
    pw = [_bdot(x, blockdiag(x)) for x in a_ab]
    m = 4
    while m < c:
        both = [_bdot(jnp.concatenate([p, t], axis=0), blockdiag(p)) for p, t in zip(pw, tinv)]
        pw = [b[:c] for b in both]
        tinv = [t + b[c:] for t, b in zip(tinv, both)]
        m *= 2
    tinv = [t + _bdot(t, blockdiag(p)) for p, t in zip(pw, tinv)]
    akv = [_bdot(x, p[4]) for x, p in zip(a_ak, pre)]
    outs = []
    for g, (lhs, _, fut, decay, vs, v) in enumerate(pre):
        xr = lax.dot_general(lhs, state.astype(BF16), _NT, preferred_element_type=F32)
        u = _bdot(tinv[g], stack(xr[:c] + akv[g]))
        outs.append(xr[c:] + _bdot(a_r[g], jnp.concatenate([stack(u), vs], axis=0)))
        upd = lax.dot_general(jnp.concatenate([u, v], axis=0).astype(BF16), fut, _TN, preferred_element_type=F32)
        state = jnp.where(row_head == lane_head, state * decay + upd, 0.0)
    return outs, state


def _wkv_prompt_body(zd_ref, sb_ref, mu_ref, w0_ref, w2_ref, a0_ref, a2_ref, g2_ref, kk_ref, ka_ref, rk_ref,
                     lnw_ref, lnb_ref, g64_ref, yd_ref, shift_ref, state_ref,
                     st_s, prev_s, r_s, lw_s, k_s, v_s, kk_s, a_s, o_s, *, tt, nt):
    t = pl.program_id(1)

    @pl.when(t == 0)
    def _():
        st_s[...] = jnp.zeros_like(st_s)
        prev_s[...] = sb_ref[0]

    x = zd_ref[0]
    first = lax.broadcasted_iota(jnp.int32, (tt, 1), 0) == 0
    prev = jnp.where(first, prev_s[...], pltpu.roll(x, 1, 0))
    last = x[tt - 1:tt]
    prev_s[...] = last
    g64 = g64_ref[...]
    r, lw, k2, v, kk, a, g = _rwkv_prep(x, prev, mu_ref[...], w0_ref[...], w2_ref[...], a0_ref[...], a2_ref[...],
                                        g2_ref[...], kk_ref[...], ka_ref[...], g64)
    r_s[...] = r
    lw_s[...] = lw
    k_s[...] = k2
    v_s[...] = v
    kk_s[...] = kk
    a_s[...] = a

    def chunk_group(c, _):
        rows = [pl.ds(pl.multiple_of((c * WKV_GROUP + u) * WKV_CHUNK, WKV_CHUNK), WKV_CHUNK)
                for u in range(WKV_GROUP)]
        outs, st = _wkv_chunks([(r_s[rw, :], lw_s[rw, :], k_s[rw, :], v_s[rw, :], kk_s[rw, :], a_s[rw, :])
                                for rw in rows], st_s[...])
        for rw, o in zip(rows, outs):
            o_s[rw, :] = o
        st_s[...] = st
        return 0

    lax.fori_loop(0, tt // (WKV_CHUNK * WKV_GROUP), chunk_group, 0)
    yd_ref[0] = _rwkv_post(o_s[...], r, k2, v, g, rk_ref[...], lnw_ref[...], lnb_ref[...], g64)

    @pl.when(t == nt - 1)
    def _():
        shift_ref[0] = last
        state_ref[0] = st_s[...]


def _rwkv_params(mu, w0, w2, a0, a2, g2, k_k, k_a, r_k, lnw, lnb):
    zeros = jnp.zeros((LORA_W, W_D), F32)
    row = lambda x: x.reshape(1, -1)
    return (row(mu), row(w0), jnp.concatenate([w2, zeros], axis=0).astype(BF16), row(a0),
            jnp.concatenate([zeros, a2], axis=0).astype(BF16), g2.astype(BF16), row(k_k), row(k_a), row(r_k),
            row(lnw), row(lnb), _group_ones(W_D, HEAD_DIM))


_RWKV_PARAM_SHAPES = ((1, N_COLS_D), (1, W_D), (LORA_W + LORA_A, W_D), (1, W_D), (LORA_W + LORA_A, W_D),
                      (LORA_G, W_D), (1, W_D), (1, W_D), (1, W_D), (1, W_D), (1, W_D), (W_D, W_D))


def _wkv_prompt(zd3, shift_buf, params, tt):
    nb, L, _ = zd3.shape
    nt = L // tt
    sc = lambda: pltpu.VMEM((tt, W_D), F32)
    return pl.pallas_call(
        functools.partial(_wkv_prompt_body, tt=tt, nt=nt),
        grid=(nb, nt),
        in_specs=[pl.BlockSpec((1, tt, N_COLS_D), lambda b, t: (b, t, 0)),
                  pl.BlockSpec((1, 1, N_COLS_D), lambda b, t: (b, 0, 0))]
                 + [_full(s) for s in _RWKV_PARAM_SHAPES],
        out_specs=[pl.BlockSpec((1, tt, W_D), lambda b, t: (b, t, 0)),
                   pl.BlockSpec((1, 1, N_COLS_D), lambda b, t: (b, 0, 0)),
                   pl.BlockSpec((1, W_D, W_D), lambda b, t: (b, 0, 0))],
        out_shape=[jax.ShapeDtypeStruct((nb, L, W_D), F32), jax.ShapeDtypeStruct((nb, 1, N_COLS_D), F32),
                   jax.ShapeDtypeStruct((nb, W_D, W_D), F32)],
        scratch_shapes=[pltpu.VMEM((W_D, W_D), F32), pltpu.VMEM((1, N_COLS_D), F32)] + [sc() for _ in range(7)],
        compiler_params=_cparams("parallel", "arbitrary"),
        name="wkv_prompt",
    )(zd3, shift_buf, *params)


def _wkv_decode_body(zd_ref, sb_ref, s0_ref, mu_ref, w0_ref, w2_ref, a0_ref, a2_ref, g2_ref, kk_ref, ka_ref, rk_ref,
                     lnw_ref, lnb_ref, g64_ref, yd_ref, shift_ref, state_ref, *, bb, L):
    x3 = zd_ref[...]
    tpos = lax.broadcasted_iota(jnp.int32, (1, L, 1), 1)
    prev3 = jnp.where(tpos == 0, sb_ref[...], pltpu.roll(x3, 1, 1))
    shift_ref[...] = x3[:, L - 1:L, :]
    g64 = g64_ref[...]
    flat = lambda y: y.reshape(bb * L, y.shape[-1])
    r, lw, k2, v, kk, a, g = _rwkv_prep(flat(x3), flat(prev3), mu_ref[...], w0_ref[...], w2_ref[...], a0_ref[...],
                                        a2_ref[...], g2_ref[...], kk_ref[...], ka_ref[...], g64)
    cube = lambda y: y.reshape(bb, L, W_D)
    r3, w3, k3, v3, kk3, b3 = cube(r), cube(jnp.exp(lw)), cube(k2), cube(v), cube(kk), cube(kk * a)
    eye = (lax.broadcasted_iota(jnp.int32, (HEAD_DIM, W_D), 0)
           == lax.broadcasted_iota(jnp.int32, (HEAD_DIM, W_D), 1) % HEAD_DIM).astype(F32)[None]

    def head_sum(y3):
        return _gsum(y3.reshape(bb * HEAD_DIM, W_D), g64).reshape(bb, HEAD_DIM, W_D)

    s = jnp.concatenate([s0_ref[:, hh] for hh in range(H_D)], axis=-1)
    outs = []
    for t in range(L):
        tok = lambda y: y[:, t:t + 1, :]
        sa = head_sum(s * -tok(kk3))
        vcol = head_sum(eye * tok(v3))
        s = s * tok(w3) + sa * tok(b3) + vcol * tok(k3)
        outs.append(jnp.sum(eye * head_sum(s * tok(r3)), axis=1, keepdims=True))
    for hh in range(H_D):
        state_ref[:, hh] = s[:, :, hh * HEAD_DIM:(hh + 1) * HEAD_DIM]
    o3 = jnp.concatenate(outs, axis=1)
    y = _rwkv_post(flat(o3), r, k2, v, g, rk_ref[...], lnw_ref[...], lnb_ref[...], g64)
    yd_ref[...] = y.reshape(bb, L, W_D)


def _wkv_decode(zd3, shift_buf, s0, params, bb):
    nb, L, _ = zd3.shape
    blk = lambda *s: pl.BlockSpec((bb,) + s, lambda i: (i,) + (0,) * len(s))
    return pl.pallas_call(
        functools.partial(_wkv_decode_body, bb=bb, L=L),
        grid=(nb // bb,),
        in_specs=[blk(L, N_COLS_D), blk(1, N_COLS_D), blk(H_D, HEAD_DIM, HEAD_DIM)]
                 + [_full(s) for s in _RWKV_PARAM_SHAPES],
        out_specs=[blk(L, W_D), blk(1, N_COLS_D), blk(H_D, HEAD_DIM, HEAD_DIM)],
        out_shape=[jax.ShapeDtypeStruct((nb, L, W_D), F32), jax.ShapeDtypeStruct((nb, 1, N_COLS_D), F32),
                   jax.ShapeDtypeStruct((nb, H_D, HEAD_DIM, HEAD_DIM), F32)],
        compiler_params=_cparams("parallel"),
        name="wkv_decode",
    )(zd3, shift_buf, s0, *params)


def _outproj_body(h_ref, ya_ref, yb_ref, yc_ref, yd_ref, w_ref, o_ref, *, yc_transposed):
    acc = h_ref[...]
    acc = acc + _bdot(ya_ref[...], w_ref[0:W_A, :])
    acc = acc + _bdot(yb_ref[...], w_ref[W_A:W_A + W_B, :])
    off = W_A + W_B
    yc = yc_ref[...].T if yc_transposed else yc_ref[...]
    acc = acc + _bdot(yc, w_ref[off:off + W_C, :])
    acc = acc + _bdot(yd_ref[...], w_ref[off + W_C:, :])
    o_ref[...] = acc


def _outproj(h, ya, yb, yc, yd, w_bf, tm, yc_transposed):
    t = h.shape[0]
    row = lambda n: pl.BlockSpec((tm, n), lambda i: (i, 0))
    ycs = pl.BlockSpec((W_C, tm), lambda i: (0, i)) if yc_transposed else row(W_C)
    return pl.pallas_call(
        functools.partial(_outproj_body, yc_transposed=yc_transposed),
        grid=(t // tm,),
        in_specs=[row(D_MODEL), row(W_A), row(W_B), ycs, row(W_D), _full((D_MODEL, D_MODEL))],
        out_specs=row(D_MODEL),
        out_shape=jax.ShapeDtypeStruct((t, D_MODEL), F32),
        compiler_params=_cparams("parallel"),
        name="outproj",
    )(h, ya, yb, yc, yd, w_bf)


def _ffn_body(blk_ref, h_ref, g_ref, w1_ref, w3_ref, w2_ref, o_ref, hn_s, acc_s, *, residual):
    del blk_ref
    j = pl.program_id(1)

    @pl.when(j == 0)
    def _():
        x = h_ref[...]
        hn_s[...] = (x * lax.rsqrt(jnp.mean(x * x, axis=-1, keepdims=True) + NORM_EPS) * g_ref[...]).astype(BF16)
        acc_s[...] = x if residual else jnp.zeros_like(x)

    hn = hn_s[...]
    a = jnp.dot(hn, w1_ref[0], preferred_element_type=F32)
    b = jnp.dot(hn, w3_ref[0], preferred_element_type=F32)
    acc_s[...] += _bdot(jax.nn.silu(a) * b, w2_ref[0])

    @pl.when(j == pl.num_programs(1) - 1)
    def _():
        o_ref[...] = acc_s[...]


def _ffn(blk_e, x, g, w1, w3, w2, tm, tf, residual):
    t = x.shape[0]
    ff = w1.shape[2]
    grid_spec = pltpu.PrefetchScalarGridSpec(
        num_scalar_prefetch=1,
        grid=(t // tm, ff // tf),
        in_specs=[pl.BlockSpec((tm, D_MODEL), lambda i, j, e: (i, 0)),
                  pl.BlockSpec((1, D_MODEL), lambda i, j, e: (0, 0)),
                  pl.BlockSpec((1, D_MODEL, tf), lambda i, j, e: (e[i], 0, j)),
                  pl.BlockSpec((1, D_MODEL, tf), lambda i, j, e: (e[i], 0, j)),
                  pl.BlockSpec((1, tf, D_MODEL), lambda i, j, e: (e[i], j, 0))],
        out_specs=pl.BlockSpec((tm, D_MODEL), lambda i, j, e: (i, 0)),
        scratch_shapes=[pltpu.VMEM((tm, D_MODEL), BF16), pltpu.VMEM((tm, D_MODEL), F32)],
    )
    return pl.pallas_call(
        functools.partial(_ffn_body, residual=residual),
        grid_spec=grid_spec,
        out_shape=jax.ShapeDtypeStruct((t, D_MODEL), F32),
        compiler_params=_cparams("parallel", "arbitrary"),
        name="ffn" if residual else "moe_ffn",
    )(blk_e, x, g.reshape(1, D_MODEL), w1, w3, w2)


def _router_body(h_ref, g_ref, w_ref, lg_ref):
    x = h_ref[...]
    hn = x * lax.rsqrt(jnp.mean(x * x, axis=-1, keepdims=True) + NORM_EPS) * g_ref[...]
    lg_ref[...] = _hdot(hn, w_ref[...])


def _router(h, g, w_pad, tm):
    t = h.shape[0]
    row = lambda n: pl.BlockSpec((tm, n), lambda i: (i, 0))
    return pl.pallas_call(
        _router_body,
        grid=(t // tm,),
        in_specs=[row(D_MODEL), _full((1, D_MODEL)), _full((D_MODEL, 128))],
        out_specs=row(128),
        out_shape=jax.ShapeDtypeStruct((t, 128), F32),
        compiler_params=_cparams("parallel"),
        name="router",
    )(h, g.reshape(1, D_MODEL), w_pad)


def _moe(h, g, router, w1, w3, w2, tm, tf):
    t = h.shape[0]
    logits = _router(h, g, jnp.pad(router, ((0, 0), (0, 128 - N_EXPERTS))), tm)
    top_logit, top_e = lax.top_k(logits[:, :N_EXPERTS], TOP_K)
    gates = jax.nn.softmax(top_logit, axis=-1)
    tk = t * TOP_K
    flat_e = top_e.reshape(tk)
    onehot = (flat_e[:, None] == jnp.arange(N_EXPERTS)[None, :]).astype(jnp.int32)
    rank = jnp.take_along_axis(jnp.cumsum(onehot, axis=0) - onehot, flat_e[:, None], axis=1)[:, 0]
    nblk = (jnp.sum(onehot, axis=0) + tm - 1) // tm
    blk_end = jnp.cumsum(nblk)
    dest = (blk_end - nblk)[flat_e] * tm + rank
    nb = -(-tk // tm) + N_EXPERTS
    row_tok = jnp.full((nb * tm,), t, jnp.int32).at[dest].set(jnp.arange(tk, dtype=jnp.int32) // TOP_K)
    blk_e = jnp.minimum(jnp.searchsorted(blk_end, jnp.arange(nb), side='right'), N_EXPERTS - 1).astype(jnp.int32)
    xg = jnp.concatenate([h, jnp.zeros((1, D_MODEL), F32)], axis=0)[row_tok]
    yb = _ffn(blk_e, xg, g, w1, w3, w2, tm, tf, False)
    d2 = dest.reshape(t, TOP_K)
    return h + (yb[d2[:, 0]] * gates[:, 0:1] + yb[d2[:, 1]] * gates[:, 1:2])


def _ple_body(h_ref, g_ref, wg_ref, p_ref, wp_ref, o_ref):
    x = h_ref[...]
    hn = (x * lax.rsqrt(jnp.mean(x * x, axis=-1, keepdims=True) + NORM_EPS) * g_ref[...]).astype(BF16)
    gate = jax.nn.sigmoid(jnp.dot(hn, wg_ref[...], preferred_element_type=F32))
    o_ref[...] = x + gate * _bdot(p_ref[...], wp_ref[...])


def _ple(h, g, wg_bf, p, wp_bf, tm):
    t = h.shape[0]
    row = lambda n: pl.BlockSpec((tm, n), lambda i: (i, 0))
    return pl.pallas_call(
        _ple_body,
        grid=(t // tm,),
        in_specs=[row(D_MODEL), _full((1, D_MODEL)), _full((D_MODEL, D_MODEL)), row(PLE_DIM),
                  _full((PLE_DIM, D_MODEL))],
        out_specs=row(D_MODEL),
        out_shape=jax.ShapeDtypeStruct((t, D_MODEL), F32),
        compiler_params=_cparams("parallel"),
        name="ple",
    )(h, g.reshape(1, D_MODEL), wg_bf, p, wp_bf)


def _row_tile(t, want):
    tm = min(t, want)
    assert t % tm == 0, (t, tm)
    return tm


def _run(x, p, decode, wts, cache, state_pool, state_shift, state_wkv, page_table):
    nb, L, _ = x.shape
    t = nb * L
    tm = _row_tile(t, 512)
    n_past = page_table.shape[1] * cache[0].shape[3] if decode else 0
    h = x.reshape(t, D_MODEL)
    ks, vs, pools, shifts, wkvs, chunk_vs = [], [], [], [], [], []
    causal = jnp.tril(jnp.ones((CHUNK, CHUNK), bool))
    for l in range(DEPTH):
        w = wts[l]
        za, zb, qn, kn, v, zd, qh, kh, vh = _inproj(h, w["norm_mix_g"], w["w_in"], w["c_qnorm_g"], w["c_knorm_g"], tm)

        ws = jnp.where(causal, w["a_ws"], 0.0)
        bs = w["a_bs"]
        if decode:
            reps = CHUNK // L
            ws = jax.vmap(lambda m: jnp.kron(jnp.eye(reps, dtype=F32), m[:L, :L]))(ws)
            bs = jnp.tile(bs[:, :L], (1, reps))
        wcat = jnp.concatenate([ws[hh] for hh in range(H_A)], axis=1).astype(BF16)
        bias = jnp.repeat(bs.T, HEAD_DIM, axis=1)
        ya, va = _mix_a(za, w["a_vnorm_g"], wcat, bias, _row_tile(t, 512) // CHUNK)

        if decode:
            buf16 = jnp.pad(state_pool[l], ((0, 0), (POOL_HALO - POOL_BUF, 0), (0, 0)))
        else:
            buf16 = jnp.zeros((nb, POOL_HALO, W_B), F32)
        proj_bd = jax.scipy.linalg.block_diag(*[w["b_proj"][gi] for gi in range(len(POOL_WINDOWS))]).astype(BF16)
        tl_b = _row_tile(L, 512)
        yb, pool16 = _mix_b(zb.reshape(nb, L, W_B), buf16, proj_bd, w["b_scale"], _row_tile(nb, 512 // tl_b), tl_b,
                            n_past)
        yb = yb.reshape(t, W_B)

        lam_init = 0.8 - 0.6 * math.exp(-0.3 * l)
        lp = w["c_lambda"]
        lam = (jnp.exp(jnp.sum(lp[0] * lp[1])) - jnp.exp(jnp.sum(lp[2] * lp[3])) + lam_init).reshape(1)
        if decode:
            yc = _attn_decode(page_table, lam, qn, kn, v, w["c_subln_g"], cache[0], cache[1], l, L, 1.0 - lam_init)
        else:
            tq = _row_tile(L, ATTN_TQ)
            yc = _attn_prompt(lam, qh, kh, vh, w["c_subln_g"], nb, L, tq, _row_tile(L, max(tq, ATTN_TK)), ATTN_NSUB,
                              1.0 - lam_init)

        params = _rwkv_params(w["d_mu"], w["d_w0"], w["d_w2"], w["d_a0"], w["d_a2"], w["d_g2"], w["d_kk"], w["d_ka"],
                              w["d_rk"].reshape(-1), w["d_lnx_w"], w["d_lnx_b"])
        zd3 = zd.reshape(nb, L, N_COLS_D)
        if decode:
            yd, shift_new, wkv_new = _wkv_decode(zd3, state_shift[l], state_wkv[l], params, _row_tile(nb, 16))
        else:
            yd, shift_new, st = _wkv_prompt(zd3, jnp.zeros((nb, 1, N_COLS_D), F32), params, _row_tile(L, 512))
            st = st.reshape(nb, H_D, HEAD_DIM, H_D, HEAD_DIM)
            wkv_new = jnp.stack([st[:, hh, :, hh, :] for hh in range(H_D)], axis=1)
        h = _outproj(h, ya, yb, yc, yd.reshape(t, W_D), w["w_out"], tm, yc_transposed=not decode)

        if l % 2 == 0:
            h = _ffn(jnp.zeros((t // tm,), jnp.int32), h, w["norm_ffn_g"], w["ffn_w1"], w["ffn_w3"], w["ffn_w2"],
                     tm, D_FF // 2, True)
        else:
            h = _moe(h, w["norm_ffn_g"], w["moe_router"], w["moe_w1"], w["moe_w3"], w["moe_w2"], tm, D_FF // 2)

        h = _ple(h, w["ple_norm_g"], w["ple_gate"], p[l].reshape(t, PLE_DIM), w["ple_proj"], tm)

        ks.append(kn.reshape(nb, L, H_C, 2, DK_C))
        vs.append(v.reshape(nb, L, H_C, HEAD_DIM))
        pools.append(pool16[:, POOL_HALO - POOL_BUF:])
        shifts.append(shift_new)
        wkvs.append(wkv_new)
        chunk_vs.append(va.reshape(nb, L, W_A))
    return (h.reshape(nb, L, D_MODEL), jnp.stack(ks), jnp.stack(vs), jnp.stack(pools), jnp.stack(shifts),
            jnp.stack(wkvs), jnp.stack(chunk_vs))


def kernel(x_prompt, x_sample, cache_k, cache_v, state_pool, state_shift, state_wkv, page_table, p_prompt, p_sample, norm_mix_g, w_in, w_out, a_vnorm_g, a_ws, a_bs, b_proj, b_scale, c_qnorm_g, c_knorm_g, c_lambda, c_subln_g, d_mu, d_w0, d_w2, d_a0, d_a2, d_g2, d_kk, d_ka, d_rk, d_lnx_w, d_lnx_b, norm_ffn_g, ffn_w1, ffn_w3, ffn_w2, moe_router, moe_w1, moe_w3, moe_w2, ple_norm_g, ple_gate, ple_proj):
    bf = lambda a: a.astype(BF16)
    wts = []
    for l in range(DEPTH):
        w = dict(norm_mix_g=norm_mix_g[l], w_in=bf(w_in[l]), w_out=bf(w_out[l]), a_vnorm_g=a_vnorm_g[l],
                 a_ws=a_ws[l], a_bs=a_bs[l], b_proj=b_proj[l], b_scale=b_scale[l], c_qnorm_g=c_qnorm_g[l],
                 c_knorm_g=c_knorm_g[l], c_lambda=c_lambda[l], c_subln_g=c_subln_g[l], d_mu=d_mu[l], d_w0=d_w0[l],
                 d_w2=d_w2[l], d_a0=d_a0[l], d_a2=d_a2[l], d_g2=d_g2[l], d_kk=d_kk[l], d_ka=d_ka[l], d_rk=d_rk[l],
                 d_lnx_w=d_lnx_w[l], d_lnx_b=d_lnx_b[l], norm_ffn_g=norm_ffn_g[l], ple_norm_g=ple_norm_g[l],
                 ple_gate=bf(ple_gate[l]), ple_proj=bf(ple_proj[l]))
        if l % 2 == 0:
            w.update(ffn_w1=bf(ffn_w1[l // 2])[None], ffn_w3=bf(ffn_w3[l // 2])[None], ffn_w2=bf(ffn_w2[l // 2])[None])
        else:
            w.update(moe_router=moe_router[l // 2], moe_w1=bf(moe_w1[l // 2]), moe_w3=bf(moe_w3[l // 2]),
                     moe_w2=bf(moe_w2[l // 2]))
        wts.append(w)
    depth, n_phys, page = cache_k.shape[:3]
    cache = (jnp.transpose(cache_k, (0, 1, 3, 4, 5, 2)).reshape(depth, n_phys, Q_C, page),
             jnp.transpose(cache_v, (0, 1, 3, 4, 2)).reshape(depth, n_phys, W_C, page))
    y_p, k_p, v_p, pool_p, shift_p, wkv_p, _ = _run(x_prompt, p_prompt, False, wts, cache, state_pool, state_shift,
                                                    state_wkv, page_table)
    y_s, k_s, v_s, pool_s, shift_s, wkv_s, cv_s = _run(x_sample, p_sample, True, wts, cache, state_pool, state_shift,
                                                       state_wkv, page_table)
    return (y_p, y_s, k_p, v_p, pool_p, shift_p, wkv_p, k_s, v_s, pool_s, shift_s, wkv_s, cv_s)
```

```python
import functools
import math

import jax
import jax.numpy as jnp
from jax import lax
from jax.experimental import pallas as pl
from jax.experimental.pallas import tpu as pltpu

F32 = jnp.float32
BF16 = jnp.bfloat16

D_MODEL = 1024
DEPTH = 4
HEAD_DIM = 64
W_A = W_B = W_C = W_D = 256
H_A = H_C = H_D = 4
CHUNK = 128
POOL_WINDOWS = (2, 4, 8, 16)
POOL_BUF = 15
POOL_HALO = 16
DK_C = 32
Q_C = 256
LORA_W = 64
LORA_A = 64
LORA_G = 128
N_COLS_D = 3 * W_D + LORA_W + LORA_A + LORA_G
OFF_A = 0
OFF_B = 512
OFF_C = 768
OFF_D = 1536
N_IN = 2560
D_FF = 2816
N_EXPERTS = 8
TOP_K = 2
PLE_DIM = 256
NORM_EPS = 1e-6
RWKV_GN_EPS = 64e-5
NEG_BIG = -1e30
LOG2_E = 1.4426950408889634

ATTN_TQ, ATTN_TK, ATTN_NSUB = 1024, 1024, 2
WKV_GROUP = 8
WKV_CHUNK = 64
VMEM_LIMIT = 48 * 1024 * 1024

_NT = (((1,), (1,)), ((), ()))
_TN = (((0,), (0,)), ((), ()))


def _cparams(*sem):
    return pltpu.CompilerParams(dimension_semantics=sem, vmem_limit_bytes=VMEM_LIMIT)


def _bdot(a, b):
    return jnp.dot(a.astype(BF16), b.astype(BF16), preferred_element_type=F32)


def _hdot(a, b, dims=None):
    if dims is None:
        return jnp.dot(a, b, preferred_element_type=F32, precision=lax.Precision.HIGHEST)
    return lax.dot_general(a, b, dims, preferred_element_type=F32, precision=lax.Precision.HIGHEST)


def _gsum(x, gmat):
    hi = x.astype(BF16)
    lo = (x - hi.astype(F32)).astype(BF16)
    return (jnp.dot(hi, gmat, preferred_element_type=F32) + jnp.dot(lo, gmat, preferred_element_type=F32))


def _group_ones(n, g):
    i = jnp.arange(n) // g
    return (i[:, None] == i[None, :]).astype(BF16)


def _full(shape):
    nd = len(shape)
    return pl.BlockSpec(shape, lambda *_: (0,) * nd)


def _inproj_body(h_ref, g_ref, w_ref, gq_ref, gk_ref, g32_ref,
                 za_ref, zb_ref, qn_ref, kn_ref, v_ref, zd_ref, qh_ref, kh_ref, vt_ref):
    x = h_ref[...]
    xn = (x * lax.rsqrt(jnp.mean(x * x, axis=-1, keepdims=True) + NORM_EPS) * g_ref[...]).astype(BF16)

    def proj(lo, hi):
        return jnp.dot(xn, w_ref[:, lo:hi], preferred_element_type=F32)

    za_ref[...] = proj(OFF_A, OFF_B)
    zb_ref[...] = proj(OFF_B, OFF_C)
    zd_ref[...] = proj(OFF_D, N_IN)
    zq = proj(OFF_C, OFF_C + Q_C)
    zk = proj(OFF_C + Q_C, OFF_C + 2 * Q_C)
    zv = proj(OFF_C + 2 * Q_C, OFF_D)
    g32 = g32_ref[...]
    qn = zq * lax.rsqrt(_gsum(zq * zq, g32) * (1.0 / DK_C) + NORM_EPS) * gq_ref[...]
    kn = zk * lax.rsqrt(_gsum(zk * zk, g32) * (1.0 / DK_C) + NORM_EPS) * gk_ref[...]
    qn_ref[...] = qn
    kn_ref[...] = kn
    v_ref[...] = zv
    qs = qn * (DK_C ** -0.5 * LOG2_E)
    for hh in range(H_C):
        sl = slice(hh * HEAD_DIM, (hh + 1) * HEAD_DIM)
        qh_ref[hh] = qs[:, sl].astype(BF16)
        kh_ref[hh] = kn[:, sl].astype(BF16)
    vt_ref[...] = zv.T.astype(BF16)


def _inproj(h, g, w_bf, gq, gk, tm):
    t = h.shape[0]
    row = lambda n: pl.BlockSpec((tm, n), lambda i: (i, 0))
    headm = pl.BlockSpec((H_C, tm, HEAD_DIM), lambda i: (0, i, 0))
    f = lambda n: jax.ShapeDtypeStruct((t, n), F32)
    hm = jax.ShapeDtypeStruct((H_C, t, HEAD_DIM), BF16)
    return pl.pallas_call(
        _inproj_body,
        grid=(t // tm,),
        in_specs=[row(D_MODEL), _full((1, D_MODEL)), _full((D_MODEL, N_IN)), _full((1, Q_C)), _full((1, Q_C)),
                  _full((Q_C, Q_C))],
        out_specs=[row(2 * W_A), row(W_B), row(Q_C), row(Q_C), row(W_C), row(N_COLS_D), headm, headm,
                   pl.BlockSpec((W_C, tm), lambda i: (0, i))],
        out_shape=[f(2 * W_A), f(W_B), f(Q_C), f(Q_C), f(W_C), f(N_COLS_D), hm, hm,
                   jax.ShapeDtypeStruct((W_C, t), BF16)],
        compiler_params=_cparams("parallel"),
        name="inproj",
    )(h, g.reshape(1, D_MODEL), w_bf, jnp.tile(gq, 2 * H_C).reshape(1, Q_C), jnp.tile(gk, 2 * H_C).reshape(1, Q_C),
      _group_ones(Q_C, DK_C))


def _mix_a_body(z_ref, vg_ref, g64_ref, wcat_ref, bias_ref, ya_ref, va_ref, *, nchunk):
    z = z_ref[...]
    u = jax.nn.gelu(z[:, :W_A])
    vv = jax.nn.gelu(z[:, W_A:])
    vn = vv * lax.rsqrt(_gsum(vv * vv, g64_ref[...]) * (1.0 / HEAD_DIM) + NORM_EPS) * vg_ref[...]
    va_ref[...] = vn
    head = lax.broadcasted_iota(jnp.int32, (1, W_A), 1) // HEAD_DIM
    wcat = wcat_ref[...]
    bias = bias_ref[...]
    for c in range(nchunk):
        rows = slice(c * CHUNK, (c + 1) * CHUNK)
        vc = vn[rows]
        vstack = jnp.concatenate([jnp.where(head == hh, vc, 0.0) for hh in range(H_A)], axis=0).astype(BF16)
        s = jnp.dot(wcat, vstack, preferred_element_type=F32) + bias
        ya_ref[rows, :] = u[rows] * s


def _mix_a(za, vg, wcat_bf, bias, nchunk):
    t = za.shape[0]
    tm = nchunk * CHUNK
    return pl.pallas_call(
        functools.partial(_mix_a_body, nchunk=nchunk),
        grid=(t // tm,),
        in_specs=[pl.BlockSpec((tm, 2 * W_A), lambda i: (i, 0)), _full((1, W_A)), _full((W_A, W_A)),
                  _full((CHUNK, H_A * CHUNK)), _full((CHUNK, W_A))],
        out_specs=[pl.BlockSpec((tm, W_A), lambda i: (i, 0)), pl.BlockSpec((tm, W_A), lambda i: (i, 0))],
        out_shape=[jax.ShapeDtypeStruct((t, W_A), F32), jax.ShapeDtypeStruct((t, W_A), F32)],
        compiler_params=_cparams("parallel"),
        name="mix_a",
    )(za, vg.reshape(1, W_A), _group_ones(W_A, HEAD_DIM), wcat_bf, bias)


def _mix_b_body(z_ref, buf_ref, proj_ref, scale_ref, y_ref, pool_ref, halo_ref, *, bb, tl, start_pos, nt):
    t = pl.program_id(1)

    @pl.when(t == 0)
    def _():
        halo_ref[...] = buf_ref[...]

    z = z_ref[...]
    x = jnp.concatenate([halo_ref[...], z], axis=1)
    a2 = x + pltpu.roll(x, 1, 1)
    a4 = a2 + pltpu.roll(a2, 2, 1)
    a8 = a4 + pltpu.roll(a4, 4, 1)
    a16 = a8 + pltpu.roll(a8, 8, 1)
    grp = lax.broadcasted_iota(jnp.int32, (1, 1, W_B), 2) // (W_B // len(POOL_WINDOWS))
    win = jnp.where(grp == 0, a2, jnp.where(grp == 1, a4, jnp.where(grp == 2, a8, a16)))[:, POOL_HALO:]
    wsize = jnp.where(grp == 0, 2, jnp.where(grp == 1, 4, jnp.where(grp == 2, 8, 16)))
    pos = start_pos + t * tl + lax.broadcasted_iota(jnp.int32, (1, tl, 1), 1)
    cnt = jnp.minimum(wsize, pos + 1).astype(F32)
    pooled = (win / cnt - z).reshape(bb * tl, W_B)
    y_ref[...] = (_bdot(pooled, proj_ref[...]) * scale_ref[...]).reshape(bb, tl, W_B)
    tail = x[:, tl:]
    halo_ref[...] = tail

    @pl.when(t == nt - 1)
    def _():
        pool_ref[...] = tail


def _mix_b(zb3, buf16, proj_bd_bf, scale, bb, tl, start_pos):
    nb, L, _ = zb3.shape
    nt = L // tl
    return pl.pallas_call(
        functools.partial(_mix_b_body, bb=bb, tl=tl, start_pos=start_pos, nt=nt),
        grid=(nb // bb, nt),
        in_specs=[pl.BlockSpec((bb, tl, W_B), lambda b, t: (b, t, 0)),
                  pl.BlockSpec((bb, POOL_HALO, W_B), lambda b, t: (b, 0, 0)),
                  _full((W_B, W_B)), _full((1, W_B))],
        out_specs=[pl.BlockSpec((bb, tl, W_B), lambda b, t: (b, t, 0)),
                   pl.BlockSpec((bb, POOL_HALO, W_B), lambda b, t: (b, 0, 0))],
        out_shape=[jax.ShapeDtypeStruct((nb, L, W_B), F32), jax.ShapeDtypeStruct((nb, POOL_HALO, W_B), F32)],
        scratch_shapes=[pltpu.VMEM((bb, POOL_HALO, W_B), F32)],
        compiler_params=_cparams("parallel", "arbitrary"),
        name="mix_b",
    )(zb3, buf16, proj_bd_bf, scale.reshape(1, W_B))


def _attn_body(lam_ref, q_ref, k_ref, vt_ref, g_ref, o_ref, *, tq, tk, nsub, out_scale):
    qi = pl.program_id(2)
    q = q_ref[0]
    lane = lax.broadcasted_iota(jnp.int32, (1, HEAD_DIM), 1)
    zero = jnp.zeros_like(q)
    q0 = jnp.where(lane < DK_C, q, zero)
    q1 = jnp.where(lane >= DK_C, q, zero)

    def soft(s, vt, m, l, acc):
        mn = jnp.maximum(m, jnp.max(s, axis=0, keepdims=True))
        p = jnp.exp2(s - mn)
        al = jnp.exp2(m - mn)
        return mn, al * l + jnp.sum(p, axis=0, keepdims=True), al * acc + jnp.dot(
            vt, p.astype(BF16), preferred_element_type=F32)

    ts = tk // nsub

    def block(j, carry, masked):
        m0, l0, a0, m1, l1, a1 = carry
        for u in range(nsub):
            off = pl.multiple_of(j * tk + u * ts, ts)
            kb = k_ref[0, pl.ds(off, ts), :]
            vt = vt_ref[:, pl.ds(off, ts)]
            s0 = lax.dot_general(kb, q0, _NT, preferred_element_type=F32)
            s1 = lax.dot_general(kb, q1, _NT, preferred_element_type=F32)
            if masked:
                key = off + lax.broadcasted_iota(jnp.int32, (ts, tq), 0)
                qry = qi * tq + lax.broadcasted_iota(jnp.int32, (ts, tq), 1)
                s0 = jnp.where(key <= qry, s0, NEG_BIG)
                s1 = jnp.where(key <= qry, s1, NEG_BIG)
            m0, l0, a0 = soft(s0, vt, m0, l0, a0)
            m1, l1, a1 = soft(s1, vt, m1, l1, a1)
        return m0, l0, a0, m1, l1, a1

    rowv = lambda v: jnp.full((1, tq), v, F32)
    acc = jnp.zeros((HEAD_DIM, tq), F32)
    carry = (rowv(NEG_BIG), rowv(0.0), acc, rowv(NEG_BIG), rowv(0.0), acc)
    nfull = (qi * tq) // tk
    carry = lax.fori_loop(0, nfull, functools.partial(block, masked=False), carry)
    if tq == tk:
        tri = lax.broadcasted_iota(jnp.int32, (ts, ts), 0) <= lax.broadcasted_iota(jnp.int32, (ts, ts), 1)
        strips = []
        for h in range(nsub):
            cols = slice(h * ts, (h + 1) * ts)
            m0, l0, a0, m1, l1, a1 = (x[:, cols] for x in carry)
            for u in range(h + 1):
                off = pl.multiple_of(nfull * tk + u * ts, ts)
                kb = k_ref[0, pl.ds(off, ts), :]
                vt = vt_ref[:, pl.ds(off, ts)]
                s0 = lax.dot_general(kb, q0[cols], _NT, preferred_element_type=F32)
                s1 = lax.dot_general(kb, q1[cols], _NT, preferred_element_type=F32)
                if u == h:
                    s0 = jnp.where(tri, s0, NEG_BIG)
                    s1 = jnp.where(tri, s1, NEG_BIG)
                m0, l0, a0 = soft(s0, vt, m0, l0, a0)
                m1, l1, a1 = soft(s1, vt, m1, l1, a1)
            strips.append((l0, a0, l1, a1))
        l0, a0, l1, a1 = (jnp.concatenate(x, axis=1) for x in zip(*strips))
    else:
        m0, l0, a0, m1, l1, a1 = block(nfull, carry, True)
    o = a0 / l0 - lam_ref[0] * (a1 / l1)
    o_ref[...] = o * lax.rsqrt(jnp.mean(o * o, axis=0, keepdims=True) + NORM_EPS) * g_ref[...] * out_scale


def _attn_prompt(lam, qh, kh, vt, g, nb, seq, tq, tk, nsub, out_scale):
    nq = seq // tq
    assert tk % tq == 0 and seq % tk == 0 and tk % nsub == 0, (seq, tq, tk, nsub)
    qspec = pl.BlockSpec((1, tq, HEAD_DIM), lambda b, h, i: (h, b * nq + i, 0))
    kspec = pl.BlockSpec((1, seq, HEAD_DIM), lambda b, h, i: (h, b, 0))
    vspec = pl.BlockSpec((HEAD_DIM, seq), lambda b, h, i: (h, b))
    return pl.pallas_call(
        functools.partial(_attn_body, tq=tq, tk=tk, nsub=nsub, out_scale=out_scale),
        grid=(nb, H_C, nq),
        in_specs=[pl.BlockSpec(memory_space=pltpu.SMEM), qspec, kspec, vspec, _full((HEAD_DIM, 1))],
        out_specs=pl.BlockSpec((HEAD_DIM, tq), lambda b, h, i: (h, b * nq + i)),
        out_shape=jax.ShapeDtypeStruct((W_C, nb * seq), F32),
        compiler_params=_cparams("parallel", "parallel", "parallel"),
        name="attn_prompt",
    )(lam, qh, kh, vt, g.reshape(HEAD_DIM, 1))


def _attn_decode_body(pt_ref, lam_ref, q_ref, kn_ref, vn_ref, g_ref, g64_ref, *rest, npages, nq, out_scale):
    del pt_ref
    kp = rest[:npages]
    vp = rest[npages:2 * npages]
    o_ref = rest[2 * npages]
    nrow = 2 * H_C * nq
    q = q_ref[...] * (DK_C ** -0.5)
    row = lax.broadcasted_iota(jnp.int32, (nrow, 1), 0)
    lane = lax.broadcasted_iota(jnp.int32, (1, Q_C), 1)
    qs = jnp.concatenate([q] * (2 * H_C), axis=0)
    qs = jnp.where(row // nq == lane // DK_C, qs, 0.0).astype(BF16)
    s_pages = [_bdot(qs, kp[j][0, 0]) for j in range(npages)]
    s_new = lax.dot_general(qs, kn_ref[...].astype(BF16), _NT, preferred_element_type=F32)
    kidx = lax.broadcasted_iota(jnp.int32, (1, nq), 1)
    s_new = jnp.where(row % nq >= kidx, s_new, NEG_BIG)
    m = jnp.max(s_new, axis=-1, keepdims=True)
    for s in s_pages:
        m = jnp.maximum(m, jnp.max(s, axis=-1, keepdims=True))
    p_new = jnp.exp(s_new - m)
    l = jnp.sum(p_new, axis=-1, keepdims=True)
    p_pages = []
    for s in s_pages:
        p = jnp.exp(s - m)
        l = l + jnp.sum(p, axis=-1, keepdims=True)
        p_pages.append(p)
    coef = jnp.where((row // nq) % 2 == 0, 1.0, -lam_ref[0]) / l

    def diff(p):
        pw = p * coef
        return jnp.concatenate([pw[(2 * hh) * nq:(2 * hh + 1) * nq] + pw[(2 * hh + 1) * nq:(2 * hh + 2) * nq]
                                for hh in range(H_C)], axis=0)

    out = _bdot(diff(p_new), vn_ref[...])
    for j in range(npages):
        out = out + lax.dot_general(diff(p_pages[j]).astype(BF16), vp[j][0, 0].astype(BF16), _NT,
                                    preferred_element_type=F32)
    vlane = lax.broadcasted_iota(jnp.int32, (1, W_C), 1) // HEAD_DIM
    o = jnp.zeros((nq, W_C), F32)
    for hh in range(H_C):
        o = jnp.where(vlane == hh, out[hh * nq:(hh + 1) * nq], o)
    o = o * lax.rsqrt(_gsum(o * o, g64_ref[...]) * (1.0 / HEAD_DIM) + NORM_EPS) * g_ref[...] * out_scale
    o_ref[...] = o


def _attn_decode(page_table, lam, qn, kn, vn, g, cache_kt, cache_vt, layer, nq, out_scale):
    nb, npages = page_table.shape
    page = cache_kt.shape[3]
    row = pl.BlockSpec((nq, Q_C), lambda b, pt: (b, 0))

    def page_spec(j):
        return pl.BlockSpec((1, 1, Q_C, page), lambda b, pt, j=j: (layer, pt[b, j], 0, 0))

    grid_spec = pltpu.PrefetchScalarGridSpec(
        num_scalar_prefetch=1,
        grid=(nb,),
        in_specs=[pl.BlockSpec(memory_space=pltpu.SMEM), row, row, row,
                  pl.BlockSpec((1, W_C), lambda b, pt: (0, 0)), pl.BlockSpec((W_C, W_C), lambda b, pt: (0, 0))]
                 + [page_spec(j) for j in range(npages)] * 2,
        out_specs=row,
    )
    return pl.pallas_call(
        functools.partial(_attn_decode_body, npages=npages, nq=nq, out_scale=out_scale),
        grid_spec=grid_spec,
        out_shape=jax.ShapeDtypeStruct((nb * nq, W_C), F32),
        compiler_params=_cparams("parallel"),
        name="attn_decode",
    )(page_table, lam, qn, kn, vn, jnp.tile(g, H_C).reshape(1, W_C), _group_ones(W_C, HEAD_DIM),
      *([cache_kt] * npages), *([cache_vt] * npages))


def _rwkv_prep(x, prev, mu, w0, w2p, a0, a2p, g2, k_k, k_a, g64):
    xs = x + (prev - x) * mu
    r = xs[:, :W_D]
    k = xs[:, W_D:2 * W_D]
    v = xs[:, 2 * W_D:3 * W_D]
    hwa = xs[:, 3 * W_D:3 * W_D + LORA_W + LORA_A]
    hg = xs[:, 3 * W_D + LORA_W + LORA_A:]
    w_log = -jax.nn.softplus(-(w0 + _bdot(jnp.tanh(hwa), w2p))) - 0.5
    logdecay = -jnp.exp(w_log)
    a = jax.nn.sigmoid(a0 + _bdot(hwa, a2p))
    g = _bdot(jax.nn.sigmoid(hg), g2)
    kk = k * k_k
    kk = kk / jnp.maximum(jnp.sqrt(_gsum(kk * kk, g64)), 1e-12)
    k2 = k * (1.0 + (a - 1.0) * k_a)
    return r, logdecay, k2, v, kk, a, g


def _rwkv_post(o, r, k2, v, g, rk, lnw, lnb, g64):
    inv_n = 1.0 / HEAD_DIM
    mean = _gsum(o, g64) * inv_n
    d = o - mean
    var = _gsum(d * d, g64) * inv_n
    on = d * lax.rsqrt(var + RWKV_GN_EPS) * lnw + lnb
    bonus = _gsum(r * k2 * rk, g64) * v
    return (on + bonus) * g


def _wkv_chunks(chunks, state):
    c = chunks[0][0].shape[0]
    n = H_D * c
    row = lax.broadcasted_iota(jnp.int32, (c, 1), 0)
    lane_head = lax.broadcasted_iota(jnp.int32, (1, W_D), 1) // HEAD_DIM
    row_head = lax.broadcasted_iota(jnp.int32, (W_D, 1), 0) // HEAD_DIM
    row_blk = lax.broadcasted_iota(jnp.int32, (n, 1), 0) // c
    col_blk = lax.broadcasted_iota(jnp.int32, (1, n), 1) // c
    ti = lax.broadcasted_iota(jnp.int32, (c, n), 0)
    si = lax.broadcasted_iota(jnp.int32, (c, n), 1) % c
    strict = si < ti
    incl = si <= ti

    def stack(y):
        return jnp.concatenate([jnp.where(lane_head == hh, y, 0.0) for hh in range(H_D)], axis=0).astype(BF16)

    def blockdiag(p):
        return jnp.where(row_blk == col_blk, jnp.concatenate([p] * H_D, axis=0), 0.0).astype(BF16)

    pre = []
    for r, lw, k, v, kk, a in chunks:
        cw = lw
        sh = 1
        while sh < c:
            cw = cw + jnp.where(row >= sh, pltpu.roll(cw, sh, 0), 0.0)
            sh *= 2
        tot = cw[c - 1:c]
        e_out = jnp.exp(-cw)
        e_fut = jnp.exp(tot - cw)
        beta = kk * a
        lhs = jnp.concatenate([-kk * jnp.exp(cw - lw), r * jnp.exp(cw)], axis=0).astype(BF16)
        rhs = jnp.concatenate([stack(beta * e_out), stack(k * e_out)], axis=0)
        fut = jnp.concatenate([beta * e_fut, k * e_fut], axis=0).astype(BF16)
        pre.append((lhs, rhs, fut, jnp.exp(tot), stack(v), v))
    amats = [lax.dot_general(lhs, rhs, _NT, preferred_element_type=F32) for lhs, rhs, *_ in pre]
    a_ab = [jnp.where(strict, am[:c, :n], 0.0) for am in amats]
    a_ak = [jnp.where(strict, am[:c, n:], 0.0) for am in amats]
    a_r = [jnp.concatenate([jnp.where(incl, am[c:, :n], 0.0), jnp.where(incl, am[c:, n:], 0.0)], axis=1)
           for am in amats]
    tinv = [(si == ti).astype(F32) + x for x in a_ab]
    pw = [_bdot(x, blockdiag(x)) for x in a_ab]
    m = 4
    while m < c:
        both = [_bdot(jnp.concatenate([p, t], axis=0), blockdiag(p)) for p, t in zip(pw, tinv)]
        pw = [b[:c] for b in both]
        tinv = [t + b[c:] for t, b in zip(tinv, both)]
        m *= 2
    tinv = [t + _bdot(t, blockdiag(p)) for p, t in zip(pw, tinv)]
    akv = [_bdot(x, p[4]) for x, p in zip(a_ak, pre)]
    outs = []
    for g, (lhs, _, fut, decay, vs, v) in enumerate(pre):
        xr = lax.dot_general(lhs, state.astype(BF16), _NT, preferred_element_type=F32)
        u = _bdot(tinv[g], stack(xr[:c] + akv[g]))
        outs.append(xr[c:] + _bdot(a_r[g], jnp.concatenate([stack(u), vs], axis=0)))
        upd = lax.dot_general(jnp.concatenate([u, v], axis=0).astype(BF16), fut, _TN, preferred_element_type=F32)
        state = jnp.where(row_head == lane_head, state * decay + upd, 0.0)
    return outs, state


def _wkv_prompt_body(zd_ref, sb_ref, mu_ref, w0_ref, w2_ref, a0_ref, a2_ref, g2_ref, kk_ref, ka_ref, rk_ref,
                     lnw_ref, lnb_ref, g64_ref, yd_ref, shift_ref, state_ref,
                     st_s, prev_s, r_s, lw_s, k_s, v_s, kk_s, a_s, o_s, *, tt, nt):
    t = pl.program_id(1)

    @pl.when(t == 0)
    def _():
        st_s[...] = jnp.zeros_like(st_s)
        prev_s[...] = sb_ref[0]

    x = zd_ref[0]
    first = lax.broadcasted_iota(jnp.int32, (tt, 1), 0) == 0
    prev = jnp.where(first, prev_s[...], pltpu.roll(x, 1, 0))
    last = x[tt - 1:tt]
    prev_s[...] = last
    g64 = g64_ref[...]
    r, lw, k2, v, kk, a, g = _rwkv_prep(x, prev, mu_ref[...], w0_ref[...], w2_ref[...], a0_ref[...], a2_ref[...],
                                        g2_ref[...], kk_ref[...], ka_ref[...], g64)
    r_s[...] = r
    lw_s[...] = lw
    k_s[...] = k2
    v_s[...] = v
    kk_s[...] = kk
    a_s[...] = a

    def chunk_group(c, _):
        rows = [pl.ds(pl.multiple_of((c * WKV_GROUP + u) * WKV_CHUNK, WKV_CHUNK), WKV_CHUNK)
                for u in range(WKV_GROUP)]
        outs, st = _wkv_chunks([(r_s[rw, :], lw_s[rw, :], k_s[rw, :], v_s[rw, :], kk_s[rw, :], a_s[rw, :])
                                for rw in rows], st_s[...])
        for rw, o in zip(rows, outs):
            o_s[rw, :] = o
        st_s[...] = st
        return 0

    lax.fori_loop(0, tt // (WKV_CHUNK * WKV_GROUP), chunk_group, 0)
    yd_ref[0] = _rwkv_post(o_s[...], r, k2, v, g, rk_ref[...], lnw_ref[...], lnb_ref[...], g64)

    @pl.when(t == nt - 1)
    def _():
        shift_ref[0] = last
        state_ref[0] = st_s[...]


def _rwkv_params(mu, w0, w2, a0, a2, g2, k_k, k_a, r_k, lnw, lnb):
    zeros = jnp.zeros((LORA_W, W_D), F32)
    row = lambda x: x.reshape(1, -1)
    return (row(mu), row(w0), jnp.concatenate([w2, zeros], axis=0).astype(BF16), row(a0),
            jnp.concatenate([zeros, a2], axis=0).astype(BF16), g2.astype(BF16), row(k_k), row(k_a), row(r_k),
            row(lnw), row(lnb), _group_ones(W_D, HEAD_DIM))


_RWKV_PARAM_SHAPES = ((1, N_COLS_D), (1, W_D), (LORA_W + LORA_A, W_D), (1, W_D), (LORA_W + LORA_A, W_D),
                      (LORA_G, W_D), (1, W_D), (1, W_D), (1, W_D), (1, W_D), (1, W_D), (W_D, W_D))


def _wkv_prompt(zd3, shift_buf, params, tt):
    nb, L, _ = zd3.shape
    nt = L // tt
    sc = lambda: pltpu.VMEM((tt, W_D), F32)
    return pl.pallas_call(
        functools.partial(_wkv_prompt_body, tt=tt, nt=nt),
        grid=(nb, nt),
        in_specs=[pl.BlockSpec((1, tt, N_COLS_D), lambda b, t: (b, t, 0)),
                  pl.BlockSpec((1, 1, N_COLS_D), lambda b, t: (b, 0, 0))]
                 + [_full(s) for s in _RWKV_PARAM_SHAPES],
        out_specs=[pl.BlockSpec((1, tt, W_D), lambda b, t: (b, t, 0)),
                   pl.BlockSpec((1, 1, N_COLS_D), lambda b, t: (b, 0, 0)),
                   pl.BlockSpec((1, W_D, W_D), lambda b, t: (b, 0, 0))],
        out_shape=[jax.ShapeDtypeStruct((nb, L, W_D), F32), jax.ShapeDtypeStruct((nb, 1, N_COLS_D), F32),
                   jax.ShapeDtypeStruct((nb, W_D, W_D), F32)],
        scratch_shapes=[pltpu.VMEM((W_D, W_D), F32), pltpu.VMEM((1, N_COLS_D), F32)] + [sc() for _ in range(7)],
        compiler_params=_cparams("parallel", "arbitrary"),
        name="wkv_prompt",
    )(zd3, shift_buf, *params)


def _wkv_decode_body(zd_ref, sb_ref, s0_ref, mu_ref, w0_ref, w2_ref, a0_ref, a2_ref, g2_ref, kk_ref, ka_ref, rk_ref,
                     lnw_ref, lnb_ref, g64_ref, yd_ref, shift_ref, state_ref, *, bb, L):
    x3 = zd_ref[...]
    tpos = lax.broadcasted_iota(jnp.int32, (1, L, 1), 1)
    prev3 = jnp.where(tpos == 0, sb_ref[...], pltpu.roll(x3, 1, 1))
    shift_ref[...] = x3[:, L - 1:L, :]
    g64 = g64_ref[...]
    flat = lambda y: y.reshape(bb * L, y.shape[-1])
    r, lw, k2, v, kk, a, g = _rwkv_prep(flat(x3), flat(prev3), mu_ref[...], w0_ref[...], w2_ref[...], a0_ref[...],
                                        a2_ref[...], g2_ref[...], kk_ref[...], ka_ref[...], g64)
    cube = lambda y: y.reshape(bb, L, W_D)
    r3, w3, k3, v3, kk3, b3 = cube(r), cube(jnp.exp(lw)), cube(k2), cube(v), cube(kk), cube(kk * a)
    eye = (lax.broadcasted_iota(jnp.int32, (HEAD_DIM, W_D), 0)
           == lax.broadcasted_iota(jnp.int32, (HEAD_DIM, W_D), 1) % HEAD_DIM).astype(F32)[None]

    def head_sum(y3):
        return _gsum(y3.reshape(bb * HEAD_DIM, W_D), g64).reshape(bb, HEAD_DIM, W_D)

    s = jnp.concatenate([s0_ref[:, hh] for hh in range(H_D)], axis=-1)
    outs = []
    for t in range(L):
        tok = lambda y: y[:, t:t + 1, :]
        sa = head_sum(s * -tok(kk3))
        vcol = head_sum(eye * tok(v3))
        s = s * tok(w3) + sa * tok(b3) + vcol * tok(k3)
        outs.append(jnp.sum(eye * head_sum(s * tok(r3)), axis=1, keepdims=True))
    for hh in range(H_D):
        state_ref[:, hh] = s[:, :, hh * HEAD_DIM:(hh + 1) * HEAD_DIM]
    o3 = jnp.concatenate(outs, axis=1)
    y = _rwkv_post(flat(o3), r, k2, v, g, rk_ref[...], lnw_ref[...], lnb_ref[...], g64)
    yd_ref[...] = y.reshape(bb, L, W_D)


def _wkv_decode(zd3, shift_all, s0_all, layer, params, bb):
    nb, L, _ = zd3.shape
    blk = lambda *s: pl.BlockSpec((bb,) + s, lambda i: (i,) + (0,) * len(s))
    lblk = lambda *s: pl.BlockSpec((None, bb) + s, lambda i: (layer, i) + (0,) * len(s))
    return pl.pallas_call(
        functools.partial(_wkv_decode_body, bb=bb, L=L),
        grid=(nb // bb,),
        in_specs=[blk(L, N_COLS_D), lblk(1, N_COLS_D), lblk(H_D, HEAD_DIM, HEAD_DIM)]
                 + [_full(s) for s in _RWKV_PARAM_SHAPES],
        out_specs=[blk(L, W_D), blk(1, N_COLS_D), blk(H_D, HEAD_DIM, HEAD_DIM)],
        out_shape=[jax.ShapeDtypeStruct((nb, L, W_D), F32), jax.ShapeDtypeStruct((nb, 1, N_COLS_D), F32),
                   jax.ShapeDtypeStruct((nb, H_D, HEAD_DIM, HEAD_DIM), F32)],
        compiler_params=_cparams("parallel"),
        name="wkv_decode",
    )(zd3, shift_all, s0_all, *params)


def _outproj_body(h_ref, ya_ref, yb_ref, yc_ref, yd_ref, w_ref, o_ref, *, yc_transposed):
    acc = h_ref[...]
    acc = acc + _bdot(ya_ref[...], w_ref[0:W_A, :])
    acc = acc + _bdot(yb_ref[...], w_ref[W_A:W_A + W_B, :])
    off = W_A + W_B
    yc = yc_ref[...].T if yc_transposed else yc_ref[...]
    acc = acc + _bdot(yc, w_ref[off:off + W_C, :])
    acc = acc + _bdot(yd_ref[...], w_ref[off + W_C:, :])
    o_ref[...] = acc


def _outproj(h, ya, yb, yc, yd, w_bf, tm, yc_transposed):
    t = h.shape[0]
    row = lambda n: pl.BlockSpec((tm, n), lambda i: (i, 0))
    ycs = pl.BlockSpec((W_C, tm), lambda i: (0, i)) if yc_transposed else row(W_C)
    return pl.pallas_call(
        functools.partial(_outproj_body, yc_transposed=yc_transposed),
        grid=(t // tm,),
        in_specs=[row(D_MODEL), row(W_A), row(W_B), ycs, row(W_D), _full((D_MODEL, D_MODEL))],
        out_specs=row(D_MODEL),
        out_shape=jax.ShapeDtypeStruct((t, D_MODEL), F32),
        compiler_params=_cparams("parallel"),
        name="outproj",
    )(h, ya, yb, yc, yd, w_bf)


def _cast_body(x_ref, o_ref):
    o_ref[...] = x_ref[...].astype(BF16)


def _to_bf16(x3, tr):
    e, r, c = x3.shape
    spec = pl.BlockSpec((1, tr, c), lambda i, j: (i, j, 0))
    return pl.pallas_call(
        _cast_body,
        grid=(e, r // tr),
        in_specs=[spec],
        out_specs=spec,
        out_shape=jax.ShapeDtypeStruct(x3.shape, BF16),
        compiler_params=_cparams("parallel", "parallel"),
        name="cast_bf16",
    )(x3)


def _ffn_body(blk_ref, h_ref, g_ref, w1_ref, w3_ref, w2_ref, o_ref, hn_s, acc_s, *, residual):
    del blk_ref
    j = pl.program_id(1)

    @pl.when(j == 0)
    def _():
        x = h_ref[...]
        hn_s[...] = (x * lax.rsqrt(jnp.mean(x * x, axis=-1, keepdims=True) + NORM_EPS) * g_ref[...]).astype(BF16)
        acc_s[...] = x if residual else jnp.zeros_like(x)

    hn = hn_s[...]
    a = jnp.dot(hn, w1_ref[0], preferred_element_type=F32)
    b = jnp.dot(hn, w3_ref[0], preferred_element_type=F32)
    acc_s[...] += _bdot(jax.nn.silu(a) * b, w2_ref[0])

    @pl.when(j == pl.num_programs(1) - 1)
    def _():
        o_ref[...] = acc_s[...]


def _ffn(blk_e, x, g, w1, w3, w2, tm, tf, residual):
    t = x.shape[0]
    ff = w1.shape[2]
    grid_spec = pltpu.PrefetchScalarGridSpec(
        num_scalar_prefetch=1,
        grid=(t // tm, ff // tf),
        in_specs=[pl.BlockSpec((tm, D_MODEL), lambda i, j, e: (i, 0)),
                  pl.BlockSpec((1, D_MODEL), lambda i, j, e: (0, 0)),
                  pl.BlockSpec((1, D_MODEL, tf), lambda i, j, e: (e[i], 0, j)),
                  pl.BlockSpec((1, D_MODEL, tf), lambda i, j, e: (e[i], 0, j)),
                  pl.BlockSpec((1, tf, D_MODEL), lambda i, j, e: (e[i], j, 0))],
        out_specs=pl.BlockSpec((tm, D_MODEL), lambda i, j, e: (i, 0)),
        scratch_shapes=[pltpu.VMEM((tm, D_MODEL), BF16), pltpu.VMEM((tm, D_MODEL), F32)],
    )
    return pl.pallas_call(
        functools.partial(_ffn_body, residual=residual),
        grid_spec=grid_spec,
        out_shape=jax.ShapeDtypeStruct((t, D_MODEL), F32),
        compiler_params=_cparams("parallel", "arbitrary"),
        name="ffn" if residual else "moe_ffn",
    )(blk_e, x, g.reshape(1, D_MODEL), w1, w3, w2)


def _router_body(h_ref, g_ref, w_ref, lg_ref):
    x = h_ref[...]
    hn = x * lax.rsqrt(jnp.mean(x * x, axis=-1, keepdims=True) + NORM_EPS) * g_ref[...]
    lg_ref[...] = _hdot(hn, w_ref[...])


def _router(h, g, w_pad, tm):
    t = h.shape[0]
    row = lambda n: pl.BlockSpec((tm, n), lambda i: (i, 0))
    return pl.pallas_call(
        _router_body,
        grid=(t // tm,),
        in_specs=[row(D_MODEL), _full((1, D_MODEL)), _full((D_MODEL, 128))],
        out_specs=row(128),
        out_shape=jax.ShapeDtypeStruct((t, 128), F32),
        compiler_params=_cparams("parallel"),
        name="router",
    )(h, g.reshape(1, D_MODEL), w_pad)


def _moe(h, g, router, w1, w3, w2, expert0, tm, tf):
    t = h.shape[0]
    logits = _router(h, g, jnp.pad(router, ((0, 0), (0, 128 - N_EXPERTS))), tm)
    top_logit, top_e = lax.top_k(logits[:, :N_EXPERTS], TOP_K)
    gates = jax.nn.softmax(top_logit, axis=-1)
    tk = t * TOP_K
    flat_e = top_e.reshape(tk)
    onehot = (flat_e[:, None] == jnp.arange(N_EXPERTS)[None, :]).astype(jnp.int32)
    rank = jnp.take_along_axis(jnp.cumsum(onehot, axis=0) - onehot, flat_e[:, None], axis=1)[:, 0]
    nblk = (jnp.sum(onehot, axis=0) + tm - 1) // tm
    blk_end = jnp.cumsum(nblk)
    dest = (blk_end - nblk)[flat_e] * tm + rank
    nb = -(-tk // tm) + N_EXPERTS
    row_tok = jnp.zeros((nb * tm,), jnp.int32).at[dest].set(jnp.arange(tk, dtype=jnp.int32) // TOP_K)
    blk_e = jnp.minimum(jnp.searchsorted(blk_end, jnp.arange(nb), side='right'), N_EXPERTS - 1).astype(jnp.int32)
    xg = h[row_tok]
    yb = _ffn(blk_e + expert0, xg, g, w1, w3, w2, tm, tf, False)
    d2 = dest.reshape(t, TOP_K)
    return h + (yb[d2[:, 0]] * gates[:, 0:1] + yb[d2[:, 1]] * gates[:, 1:2])


def _ple_body(h_ref, g_ref, wg_ref, p_ref, wp_ref, o_ref):
    x = h_ref[...]
    hn = (x * lax.rsqrt(jnp.mean(x * x, axis=-1, keepdims=True) + NORM_EPS) * g_ref[...]).astype(BF16)
    gate = jax.nn.sigmoid(jnp.dot(hn, wg_ref[...], preferred_element_type=F32))
    o_ref[...] = x + gate * _bdot(p_ref[0], wp_ref[...])


def _ple(h, g, wg_bf, p_all, layer, wp_bf, tm):
    t = h.shape[0]
    row = lambda n: pl.BlockSpec((tm, n), lambda i: (i, 0))
    return pl.pallas_call(
        _ple_body,
        grid=(t // tm,),
        in_specs=[row(D_MODEL), _full((1, D_MODEL)), _full((D_MODEL, D_MODEL)),
                  pl.BlockSpec((1, tm, PLE_DIM), lambda i: (layer, i, 0)), _full((PLE_DIM, D_MODEL))],
        out_specs=row(D_MODEL),
        out_shape=jax.ShapeDtypeStruct((t, D_MODEL), F32),
        compiler_params=_cparams("parallel"),
        name="ple",
    )(h, g.reshape(1, D_MODEL), wg_bf, p_all, wp_bf)


def _row_tile(t, want):
    tm = min(t, want)
    assert t % tm == 0, (t, tm)
    return tm


def _run(x, p, decode, wts, cache, state_pool, state_shift, state_wkv, page_table):
    nb, L, _ = x.shape
    t = nb * L
    tm = _row_tile(t, 512)
    n_past = page_table.shape[1] * cache[0].shape[3] if decode else 0
    h = x.reshape(t, D_MODEL)
    ks, vs, pools, shifts, wkvs, chunk_vs = [], [], [], [], [], []
    causal = jnp.tril(jnp.ones((CHUNK, CHUNK), bool))
    for l in range(DEPTH):
        w = wts[l]
        za, zb, qn, kn, v, zd, qh, kh, vh = _inproj(h, w["norm_mix_g"], w["w_in"], w["c_qnorm_g"], w["c_knorm_g"], tm)

        ws = jnp.where(causal, w["a_ws"], 0.0)
        bs = w["a_bs"]
        if decode:
            reps = CHUNK // L
            ws = jax.vmap(lambda m: jnp.kron(jnp.eye(reps, dtype=F32), m[:L, :L]))(ws)
            bs = jnp.tile(bs[:, :L], (1, reps))
        wcat = jnp.concatenate([ws[hh] for hh in range(H_A)], axis=1).astype(BF16)
        bias = jnp.repeat(bs.T, HEAD_DIM, axis=1)
        ya, va = _mix_a(za, w["a_vnorm_g"], wcat, bias, _row_tile(t, 512) // CHUNK)

        if decode:
            buf16 = jnp.pad(state_pool[l], ((0, 0), (POOL_HALO - POOL_BUF, 0), (0, 0)))
        else:
            buf16 = jnp.zeros((nb, POOL_HALO, W_B), F32)
        proj_bd = jax.scipy.linalg.block_diag(*[w["b_proj"][gi] for gi in range(len(POOL_WINDOWS))]).astype(BF16)
        tl_b = _row_tile(L, 512)
        yb, pool16 = _mix_b(zb.reshape(nb, L, W_B), buf16, proj_bd, w["b_scale"], _row_tile(nb, 512 // tl_b), tl_b,
                            n_past)
        yb = yb.reshape(t, W_B)

        lam_init = 0.8 - 0.6 * math.exp(-0.3 * l)
        lp = w["c_lambda"]
        lam = (jnp.exp(jnp.sum(lp[0] * lp[1])) - jnp.exp(jnp.sum(lp[2] * lp[3])) + lam_init).reshape(1)
        if decode:
            yc = _attn_decode(page_table, lam, qn, kn, v, w["c_subln_g"], cache[0], cache[1], l, L, 1.0 - lam_init)
        else:
            tq = _row_tile(L, ATTN_TQ)
            yc = _attn_prompt(lam, qh, kh, vh, w["c_subln_g"], nb, L, tq, _row_tile(L, max(tq, ATTN_TK)), ATTN_NSUB,
                              1.0 - lam_init)

        params = _rwkv_params(w["d_mu"], w["d_w0"], w["d_w2"], w["d_a0"], w["d_a2"], w["d_g2"], w["d_kk"], w["d_ka"],
                              w["d_rk"].reshape(-1), w["d_lnx_w"], w["d_lnx_b"])
        zd3 = zd.reshape(nb, L, N_COLS_D)
        if decode:
            yd, shift_new, wkv_new = _wkv_decode(zd3, state_shift, state_wkv, l, params, _row_tile(nb, 16))
        else:
            yd, shift_new, st = _wkv_prompt(zd3, jnp.zeros((nb, 1, N_COLS_D), F32), params, _row_tile(L, 512))
            st = st.reshape(nb, H_D, HEAD_DIM, H_D, HEAD_DIM)
            wkv_new = jnp.stack([st[:, hh, :, hh, :] for hh in range(H_D)], axis=1)
        h = _outproj(h, ya, yb, yc, yd.reshape(t, W_D), w["w_out"], tm, yc_transposed=not decode)

        if l % 2 == 0:
            h = _ffn(jnp.full((t // tm,), w["expert0"], jnp.int32), h, w["norm_ffn_g"], w["ffn_w1"], w["ffn_w3"],
                     w["ffn_w2"], tm, D_FF // 2, True)
        else:
            h = _moe(h, w["norm_ffn_g"], w["moe_router"], w["moe_w1"], w["moe_w3"], w["moe_w2"], w["expert0"], tm,
                     D_FF // 2)

        h = _ple(h, w["ple_norm_g"], w["ple_gate"], p.reshape(DEPTH, t, PLE_DIM), l, w["ple_proj"], tm)

        ks.append(kn.reshape(nb, L, H_C, 2, DK_C))
        vs.append(v.reshape(nb, L, H_C, HEAD_DIM))
        pools.append(pool16[:, POOL_HALO - POOL_BUF:])
        shifts.append(shift_new)
        wkvs.append(wkv_new)
        chunk_vs.append(va.reshape(nb, L, W_A))
    return (h.reshape(nb, L, D_MODEL), jnp.stack(ks), jnp.stack(vs), jnp.stack(pools), jnp.stack(shifts),
            jnp.stack(wkvs), jnp.stack(chunk_vs))


def kernel(x_prompt, x_sample, cache_k, cache_v, state_pool, state_shift, state_wkv, page_table, p_prompt, p_sample, norm_mix_g, w_in, w_out, a_vnorm_g, a_ws, a_bs, b_proj, b_scale, c_qnorm_g, c_knorm_g, c_lambda, c_subln_g, d_mu, d_w0, d_w2, d_a0, d_a2, d_g2, d_kk, d_ka, d_rk, d_lnx_w, d_lnx_b, norm_ffn_g, ffn_w1, ffn_w3, ffn_w2, moe_router, moe_w1, moe_w3, moe_w2, ple_norm_g, ple_gate, ple_proj):
    bf = lambda a: a.astype(BF16)
    ffn_bf = (bf(ffn_w1), bf(ffn_w3), bf(ffn_w2))
    moe_bf = (_to_bf16(moe_w1.reshape(-1, D_MODEL, D_FF), 512), _to_bf16(moe_w3.reshape(-1, D_MODEL, D_FF), 512),
              _to_bf16(moe_w2.reshape(-1, D_FF, D_MODEL), D_FF // 4))
    wts = []
    for l in range(DEPTH):
        w = dict(norm_mix_g=norm_mix_g[l], w_in=bf(w_in[l]), w_out=bf(w_out[l]), a_vnorm_g=a_vnorm_g[l],
                 a_ws=a_ws[l], a_bs=a_bs[l], b_proj=b_proj[l], b_scale=b_scale[l], c_qnorm_g=c_qnorm_g[l],
                 c_knorm_g=c_knorm_g[l], c_lambda=c_lambda[l], c_subln_g=c_subln_g[l], d_mu=d_mu[l], d_w0=d_w0[l],
                 d_w2=d_w2[l], d_a0=d_a0[l], d_a2=d_a2[l], d_g2=d_g2[l], d_kk=d_kk[l], d_ka=d_ka[l], d_rk=d_rk[l],
                 d_lnx_w=d_lnx_w[l], d_lnx_b=d_lnx_b[l], norm_ffn_g=norm_ffn_g[l], ple_norm_g=ple_norm_g[l],
                 ple_gate=bf(ple_gate[l]), ple_proj=bf(ple_proj[l]))
        if l % 2 == 0:
            w.update(ffn_w1=ffn_bf[0], ffn_w3=ffn_bf[1], ffn_w2=ffn_bf[2], expert0=l // 2)
        else:
            w.update(moe_router=moe_router[l // 2], moe_w1=moe_bf[0], moe_w3=moe_bf[1], moe_w2=moe_bf[2],
                     expert0=N_EXPERTS * (l // 2))
        wts.append(w)
    depth, n_phys, page = cache_k.shape[:3]
    cache = (jnp.transpose(cache_k, (0, 1, 3, 4, 5, 2)).reshape(depth, n_phys, Q_C, page),
             jnp.transpose(cache_v, (0, 1, 3, 4, 2)).reshape(depth, n_phys, W_C, page))
    y_p, k_p, v_p, pool_p, shift_p, wkv_p, _ = _run(x_prompt, p_prompt, False, wts, cache, state_pool, state_shift,
                                                    state_wkv, page_table)
    y_s, k_s, v_s, pool_s, shift_s, wkv_s, cv_s = _run(x_sample, p_sample, True, wts, cache, state_pool, state_shift,
                                                       state_wkv, page_table)
    return (y_p, y_s, k_p, v_p, pool_p, shift_p, wkv_p, k_s, v_s, pool_s, shift_s, wkv_s, cv_s)
```

```python
import functools
import math

import jax
import jax.numpy as jnp
from jax import lax
from jax.experimental import pallas as pl
from jax.experimental.pallas import tpu as pltpu

F32 = jnp.float32
BF16 = jnp.bfloat16

D_MODEL = 1024
DEPTH = 4
HEAD_DIM = 64
W_A = W_B = W_C = W_D = 256
H_A = H_C = H_D = 4
CHUNK = 128
POOL_WINDOWS = (2, 4, 8, 16)
POOL_BUF = 15
POOL_HALO = 16
DK_C = 32
Q_C = 256
LORA_W = 64
LORA_A = 64
LORA_G = 128
N_COLS_D = 3 * W_D + LORA_W + LORA_A + LORA_G
OFF_A = 0
OFF_B = 512
OFF_C = 768
OFF_D = 1536
N_IN = 2560
D_FF = 2816
N_EXPERTS = 8
TOP_K = 2
PLE_DIM = 256
NORM_EPS = 1e-6
RWKV_GN_EPS = 64e-5
NEG_BIG = -1e30
LOG2_E = 1.4426950408889634

ATTN_TQ, ATTN_TK, ATTN_NSUB = 1024, 1024, 2
RANK_SEG = 512
ONES_ROWS = 16
WKV_GROUP = 8
WKV_CHUNK = 64
VMEM_LIMIT = 48 * 1024 * 1024

_NT = (((1,), (1,)), ((), ()))
_TN = (((0,), (0,)), ((), ()))


def _cparams(*sem):
    return pltpu.CompilerParams(dimension_semantics=sem, vmem_limit_bytes=VMEM_LIMIT)


def _bdot(a, b):
    return jnp.dot(a.astype(BF16), b.astype(BF16), preferred_element_type=F32)


def _hdot(a, b, dims=None):
    if dims is None:
        return jnp.dot(a, b, preferred_element_type=F32, precision=lax.Precision.HIGHEST)
    return lax.dot_general(a, b, dims, preferred_element_type=F32, precision=lax.Precision.HIGHEST)


def _gsum(x, gmat):
    hi = x.astype(BF16)
    lo = (x - hi.astype(F32)).astype(BF16)
    return (jnp.dot(hi, gmat, preferred_element_type=F32) + jnp.dot(lo, gmat, preferred_element_type=F32))


def _group_ones(n, g):
    i = jnp.arange(n) // g
    return (i[:, None] == i[None, :]).astype(BF16)


def _full(shape):
    nd = len(shape)
    return pl.BlockSpec(shape, lambda *_: (0,) * nd)


def _inproj_body(h_ref, g_ref, w_ref, gq_ref, gk_ref, g32_ref,
                 za_ref, zb_ref, qn_ref, kn_ref, v_ref, zd_ref, qh_ref, kh_ref, vt_ref):
    x = h_ref[...]
    xn = (x * lax.rsqrt(jnp.mean(x * x, axis=-1, keepdims=True) + NORM_EPS) * g_ref[...]).astype(BF16)

    def proj(lo, hi):
        return jnp.dot(xn, w_ref[:, lo:hi], preferred_element_type=F32)

    za_ref[...] = proj(OFF_A, OFF_B)
    zb_ref[...] = proj(OFF_B, OFF_C)
    zd_ref[...] = proj(OFF_D, N_IN)
    zq = proj(OFF_C, OFF_C + Q_C)
    zk = proj(OFF_C + Q_C, OFF_C + 2 * Q_C)
    zv = proj(OFF_C + 2 * Q_C, OFF_D)
    g32 = g32_ref[...]
    qn = zq * lax.rsqrt(_gsum(zq * zq, g32) * (1.0 / DK_C) + NORM_EPS) * gq_ref[...]
    kn = zk * lax.rsqrt(_gsum(zk * zk, g32) * (1.0 / DK_C) + NORM_EPS) * gk_ref[...]
    qn_ref[...] = qn
    kn_ref[...] = kn
    v_ref[...] = zv
    qs = qn * (DK_C ** -0.5 * LOG2_E)
    for hh in range(H_C):
        sl = slice(hh * HEAD_DIM, (hh + 1) * HEAD_DIM)
        qh_ref[hh] = qs[:, sl].astype(BF16)
        kh_ref[hh] = kn[:, sl].astype(BF16)
    vt_ref[...] = zv.T.astype(BF16)


def _inproj(h, g, w_bf, gq, gk, tm):
    t = h.shape[0]
    row = lambda n: pl.BlockSpec((tm, n), lambda i: (i, 0))
    headm = pl.BlockSpec((H_C, tm, HEAD_DIM), lambda i: (0, i, 0))
    f = lambda n: jax.ShapeDtypeStruct((t, n), F32)
    hm = jax.ShapeDtypeStruct((H_C, t, HEAD_DIM), BF16)
    return pl.pallas_call(
        _inproj_body,
        grid=(t // tm,),
        in_specs=[row(D_MODEL), _full((1, D_MODEL)), _full((D_MODEL, N_IN)), _full((1, Q_C)), _full((1, Q_C)),
                  _full((Q_C, Q_C))],
        out_specs=[row(2 * W_A), row(W_B), row(Q_C), row(Q_C), row(W_C), row(N_COLS_D), headm, headm,
                   pl.BlockSpec((W_C, tm), lambda i: (0, i))],
        out_shape=[f(2 * W_A), f(W_B), f(Q_C), f(Q_C), f(W_C), f(N_COLS_D), hm, hm,
                   jax.ShapeDtypeStruct((W_C, t), BF16)],
        compiler_params=_cparams("parallel"),
        name="inproj",
    )(h, g.reshape(1, D_MODEL), w_bf, jnp.tile(gq, 2 * H_C).reshape(1, Q_C), jnp.tile(gk, 2 * H_C).reshape(1, Q_C),
      _group_ones(Q_C, DK_C))


def _mix_a_body(z_ref, vg_ref, g64_ref, wcat_ref, bias_ref, ya_ref, va_ref, *, nchunk):
    z = z_ref[...]
    u = jax.nn.gelu(z[:, :W_A])
    vv = jax.nn.gelu(z[:, W_A:])
    vn = vv * lax.rsqrt(_gsum(vv * vv, g64_ref[...]) * (1.0 / HEAD_DIM) + NORM_EPS) * vg_ref[...]
    va_ref[...] = vn
    head = lax.broadcasted_iota(jnp.int32, (1, W_A), 1) // HEAD_DIM
    wcat = wcat_ref[...]
    bias = bias_ref[...]
    for c in range(nchunk):
        rows = slice(c * CHUNK, (c + 1) * CHUNK)
        vc = vn[rows]
        vstack = jnp.concatenate([jnp.where(head == hh, vc, 0.0) for hh in range(H_A)], axis=0).astype(BF16)
        s = jnp.dot(wcat, vstack, preferred_element_type=F32) + bias
        ya_ref[rows, :] = u[rows] * s


def _mix_a(za, vg, wcat_bf, bias, nchunk):
    t = za.shape[0]
    tm = nchunk * CHUNK
    return pl.pallas_call(
        functools.partial(_mix_a_body, nchunk=nchunk),
        grid=(t // tm,),
        in_specs=[pl.BlockSpec((tm, 2 * W_A), lambda i: (i, 0)), _full((1, W_A)), _full((W_A, W_A)),
                  _full((CHUNK, H_A * CHUNK)), _full((CHUNK, W_A))],
        out_specs=[pl.BlockSpec((tm, W_A), lambda i: (i, 0)), pl.BlockSpec((tm, W_A), lambda i: (i, 0))],
        out_shape=[jax.ShapeDtypeStruct((t, W_A), F32), jax.ShapeDtypeStruct((t, W_A), F32)],
        compiler_params=_cparams("parallel"),
        name="mix_a",
    )(za, vg.reshape(1, W_A), _group_ones(W_A, HEAD_DIM), wcat_bf, bias)


def _mix_b_body(z_ref, buf_ref, proj_ref, scale_ref, y_ref, pool_ref, halo_ref, *, bb, tl, start_pos, nt):
    t = pl.program_id(1)

    @pl.when(t == 0)
    def _():
        halo_ref[...] = buf_ref[...]

    z = z_ref[...]
    x = jnp.concatenate([halo_ref[...], z], axis=1)
    a2 = x + pltpu.roll(x, 1, 1)
    a4 = a2 + pltpu.roll(a2, 2, 1)
    a8 = a4 + pltpu.roll(a4, 4, 1)
    a16 = a8 + pltpu.roll(a8, 8, 1)
    grp = lax.broadcasted_iota(jnp.int32, (1, 1, W_B), 2) // (W_B // len(POOL_WINDOWS))
    win = jnp.where(grp == 0, a2, jnp.where(grp == 1, a4, jnp.where(grp == 2, a8, a16)))[:, POOL_HALO:]
    wsize = jnp.where(grp == 0, 2, jnp.where(grp == 1, 4, jnp.where(grp == 2, 8, 16)))
    pos = start_pos + t * tl + lax.broadcasted_iota(jnp.int32, (1, tl, 1), 1)
    cnt = jnp.minimum(wsize, pos + 1).astype(F32)
    pooled = (win / cnt - z).reshape(bb * tl, W_B)
    y_ref[...] = (_bdot(pooled, proj_ref[...]) * scale_ref[...]).reshape(bb, tl, W_B)
    tail = x[:, tl:]
    halo_ref[...] = tail

    @pl.when(t == nt - 1)
    def _():
        pool_ref[...] = tail


def _mix_b(zb3, buf16, proj_bd_bf, scale, bb, tl, start_pos):
    nb, L, _ = zb3.shape
    nt = L // tl
    return pl.pallas_call(
        functools.partial(_mix_b_body, bb=bb, tl=tl, start_pos=start_pos, nt=nt),
        grid=(nb // bb, nt),
        in_specs=[pl.BlockSpec((bb, tl, W_B), lambda b, t: (b, t, 0)),
                  pl.BlockSpec((bb, POOL_HALO, W_B), lambda b, t: (b, 0, 0)),
                  _full((W_B, W_B)), _full((1, W_B))],
        out_specs=[pl.BlockSpec((bb, tl, W_B), lambda b, t: (b, t, 0)),
                   pl.BlockSpec((bb, POOL_HALO, W_B), lambda b, t: (b, 0, 0))],
        out_shape=[jax.ShapeDtypeStruct((nb, L, W_B), F32), jax.ShapeDtypeStruct((nb, POOL_HALO, W_B), F32)],
        scratch_shapes=[pltpu.VMEM((bb, POOL_HALO, W_B), F32)],
        compiler_params=_cparams("parallel", "arbitrary"),
        name="mix_b",
    )(zb3, buf16, proj_bd_bf, scale.reshape(1, W_B))


def _attn_body(lam_ref, q_ref, k_ref, vt_ref, g_ref, o_ref, *, tq, tk, nsub, out_scale):
    qi = pl.program_id(2)
    q = q_ref[0]
    lane = lax.broadcasted_iota(jnp.int32, (1, HEAD_DIM), 1)
    zero = jnp.zeros_like(q)
    q0 = jnp.where(lane < DK_C, q, zero)
    q1 = jnp.where(lane >= DK_C, q, zero)

    ts = tk // nsub
    ones_rows = jnp.ones((ONES_ROWS, ts), BF16)

    def values(off):
        return jnp.concatenate([vt_ref[:, pl.ds(off, ts)], ones_rows], axis=0)

    def soft(s, vta, m, acc):
        mn = jnp.maximum(m, jnp.max(s, axis=0, keepdims=True))
        p = jnp.exp2(s - mn).astype(BF16)
        return mn, jnp.exp2(m - mn) * acc + jnp.dot(vta, p, preferred_element_type=F32)

    def block(j, carry, masked):
        m0, a0, m1, a1 = carry
        for u in range(nsub):
            off = pl.multiple_of(j * tk + u * ts, ts)
            kb = k_ref[0, pl.ds(off, ts), :]
            vta = values(off)
            s0 = lax.dot_general(kb, q0, _NT, preferred_element_type=F32)
            s1 = lax.dot_general(kb, q1, _NT, preferred_element_type=F32)
            if masked:
                key = off + lax.broadcasted_iota(jnp.int32, (ts, tq), 0)
                qry = qi * tq + lax.broadcasted_iota(jnp.int32, (ts, tq), 1)
                s0 = jnp.where(key <= qry, s0, NEG_BIG)
                s1 = jnp.where(key <= qry, s1, NEG_BIG)
            m0, a0 = soft(s0, vta, m0, a0)
            m1, a1 = soft(s1, vta, m1, a1)
        return m0, a0, m1, a1

    rowv = lambda v: jnp.full((1, tq), v, F32)
    acc = jnp.zeros((HEAD_DIM + ONES_ROWS, tq), F32)
    carry = (rowv(NEG_BIG), acc, rowv(NEG_BIG), acc)
    nfull = (qi * tq) // tk
    carry = lax.fori_loop(0, nfull, functools.partial(block, masked=False), carry)
    if tq == tk:
        tri = lax.broadcasted_iota(jnp.int32, (ts, ts), 0) <= lax.broadcasted_iota(jnp.int32, (ts, ts), 1)
        strips = []
        for h in range(nsub):
            cols = slice(h * ts, (h + 1) * ts)
            m0, a0, m1, a1 = (x[:, cols] for x in carry)
            for u in range(h + 1):
                off = pl.multiple_of(nfull * tk + u * ts, ts)
                kb = k_ref[0, pl.ds(off, ts), :]
                vta = values(off)
                s0 = lax.dot_general(kb, q0[cols], _NT, preferred_element_type=F32)
                s1 = lax.dot_general(kb, q1[cols], _NT, preferred_element_type=F32)
                if u == h:
                    s0 = jnp.where(tri, s0, NEG_BIG)
                    s1 = jnp.where(tri, s1, NEG_BIG)
                m0, a0 = soft(s0, vta, m0, a0)
                m1, a1 = soft(s1, vta, m1, a1)
            strips.append((a0, a1))
        a0, a1 = (jnp.concatenate(x, axis=1) for x in zip(*strips))
    else:
        _, a0, _, a1 = block(nfull, carry, True)
    l0, l1 = a0[HEAD_DIM:HEAD_DIM + 1], a1[HEAD_DIM:HEAD_DIM + 1]
    a0, a1 = a0[:HEAD_DIM], a1[:HEAD_DIM]
    o = a0 / l0 - lam_ref[0] * (a1 / l1)
    o_ref[...] = o * lax.rsqrt(jnp.mean(o * o, axis=0, keepdims=True) + NORM_EPS) * g_ref[...] * out_scale


def _attn_prompt(lam, qh, kh, vt, g, nb, seq, tq, tk, nsub, out_scale):
    nq = seq // tq
    assert tk % tq == 0 and seq % tk == 0 and tk % nsub == 0, (seq, tq, tk, nsub)
    qspec = pl.BlockSpec((1, tq, HEAD_DIM), lambda b, h, i: (h, b * nq + i, 0))
    kspec = pl.BlockSpec((1, seq, HEAD_DIM), lambda b, h, i: (h, b, 0))
    vspec = pl.BlockSpec((HEAD_DIM, seq), lambda b, h, i: (h, b))
    return pl.pallas_call(
        functools.partial(_attn_body, tq=tq, tk=tk, nsub=nsub, out_scale=out_scale),
        grid=(nb, H_C, nq),
        in_specs=[pl.BlockSpec(memory_space=pltpu.SMEM), qspec, kspec, vspec, _full((HEAD_DIM, 1))],
        out_specs=pl.BlockSpec((HEAD_DIM, tq), lambda b, h, i: (h, b * nq + i)),
        out_shape=jax.ShapeDtypeStruct((W_C, nb * seq), F32),
        compiler_params=_cparams("parallel", "parallel", "parallel"),
        name="attn_prompt",
    )(lam, qh, kh, vt, g.reshape(HEAD_DIM, 1))


def _attn_decode_body(pt_ref, lam_ref, q_ref, kn_ref, vn_ref, g_ref, g64_ref, *rest, npages, nq, out_scale):
    del pt_ref
    kp = rest[:npages]
    vp = rest[npages:2 * npages]
    o_ref = rest[2 * npages]
    nrow = 2 * H_C * nq
    q = q_ref[...] * (DK_C ** -0.5)
    row = lax.broadcasted_iota(jnp.int32, (nrow, 1), 0)
    lane = lax.broadcasted_iota(jnp.int32, (1, Q_C), 1)
    qs = jnp.concatenate([q] * (2 * H_C), axis=0)
    qs = jnp.where(row // nq == lane // DK_C, qs, 0.0).astype(BF16)
    s_pages = [_bdot(qs, kp[j][0, 0]) for j in range(npages)]
    s_new = lax.dot_general(qs, kn_ref[...].astype(BF16), _NT, preferred_element_type=F32)
    kidx = lax.broadcasted_iota(jnp.int32, (1, nq), 1)
    s_new = jnp.where(row % nq >= kidx, s_new, NEG_BIG)
    m = jnp.max(s_new, axis=-1, keepdims=True)
    for s in s_pages:
        m = jnp.maximum(m, jnp.max(s, axis=-1, keepdims=True))
    p_new = jnp.exp(s_new - m)
    l = jnp.sum(p_new, axis=-1, keepdims=True)
    p_pages = []
    for s in s_pages:
        p = jnp.exp(s - m)
        l = l + jnp.sum(p, axis=-1, keepdims=True)
        p_pages.append(p)
    coef = jnp.where((row // nq) % 2 == 0, 1.0, -lam_ref[0]) / l

    def diff(p):
        pw = p * coef
        return jnp.concatenate([pw[(2 * hh) * nq:(2 * hh + 1) * nq] + pw[(2 * hh + 1) * nq:(2 * hh + 2) * nq]
                                for hh in range(H_C)], axis=0)

    out = _bdot(diff(p_new), vn_ref[...])
    for j in range(npages):
        out = out + lax.dot_general(diff(p_pages[j]).astype(BF16), vp[j][0, 0].astype(BF16), _NT,
                                    preferred_element_type=F32)
    vlane = lax.broadcasted_iota(jnp.int32, (1, W_C), 1) // HEAD_DIM
    o = jnp.zeros((nq, W_C), F32)
    for hh in range(H_C):
        o = jnp.where(vlane == hh, out[hh * nq:(hh + 1) * nq], o)
    o = o * lax.rsqrt(_gsum(o * o, g64_ref[...]) * (1.0 / HEAD_DIM) + NORM_EPS) * g_ref[...] * out_scale
    o_ref[...] = o


def _attn_decode(page_table, lam, qn, kn, vn, g, cache_kt, cache_vt, layer, nq, out_scale):
    nb, npages = page_table.shape
    page = cache_kt.shape[3]
    row = pl.BlockSpec((nq, Q_C), lambda b, pt: (b, 0))

    def page_spec(j):
        return pl.BlockSpec((1, 1, Q_C, page), lambda b, pt, j=j: (layer, pt[b, j], 0, 0))

    grid_spec = pltpu.PrefetchScalarGridSpec(
        num_scalar_prefetch=1,
        grid=(nb,),
        in_specs=[pl.BlockSpec(memory_space=pltpu.SMEM), row, row, row,
                  pl.BlockSpec((1, W_C), lambda b, pt: (0, 0)), pl.BlockSpec((W_C, W_C), lambda b, pt: (0, 0))]
                 + [page_spec(j) for j in range(npages)] * 2,
        out_specs=row,
    )
    return pl.pallas_call(
        functools.partial(_attn_decode_body, npages=npages, nq=nq, out_scale=out_scale),
        grid_spec=grid_spec,
        out_shape=jax.ShapeDtypeStruct((nb * nq, W_C), F32),
        compiler_params=_cparams("parallel"),
        name="attn_decode",
    )(page_table, lam, qn, kn, vn, jnp.tile(g, H_C).reshape(1, W_C), _group_ones(W_C, HEAD_DIM),
      *([cache_kt] * npages), *([cache_vt] * npages))


def _rwkv_prep(x, prev, mu, w0, w2p, a0, a2p, g2, k_k, k_a, g64):
    xs = x + (prev - x) * mu
    r = xs[:, :W_D]
    k = xs[:, W_D:2 * W_D]
    v = xs[:, 2 * W_D:3 * W_D]
    hwa = xs[:, 3 * W_D:3 * W_D + LORA_W + LORA_A]
    hg = xs[:, 3 * W_D + LORA_W + LORA_A:]
    logdecay = -math.exp(-0.5) * jax.nn.sigmoid(w0 + _bdot(jnp.tanh(hwa), w2p))
    a = jax.nn.sigmoid(a0 + _bdot(hwa, a2p))
    g = _bdot(jax.nn.sigmoid(hg), g2)
    kk = k * k_k
    kk = kk / jnp.maximum(jnp.sqrt(_gsum(kk * kk, g64)), 1e-12)
    k2 = k * (1.0 + (a - 1.0) * k_a)
    return r, logdecay, k2, v, kk, a, g


def _rwkv_post(o, r, k2, v, g, rk, lnw, lnb, g64):
    inv_n = 1.0 / HEAD_DIM
    mean = _gsum(o, g64) * inv_n
    d = o - mean
    var = _gsum(d * d, g64) * inv_n
    on = d * lax.rsqrt(var + RWKV_GN_EPS) * lnw + lnb
    bonus = _gsum(r * k2 * rk, g64) * v
    return (on + bonus) * g


def _wkv_chunks(chunks, state):
    c = chunks[0][0].shape[0]
    n = H_D * c
    row = lax.broadcasted_iota(jnp.int32, (c, 1), 0)
    lane_head = lax.broadcasted_iota(jnp.int32, (1, W_D), 1) // HEAD_DIM
    row_head = lax.broadcasted_iota(jnp.int32, (W_D, 1), 0) // HEAD_DIM
    row_blk = lax.broadcasted_iota(jnp.int32, (n, 1), 0) // c
    col_blk = lax.broadcasted_iota(jnp.int32, (1, n), 1) // c
    ti = lax.broadcasted_iota(jnp.int32, (c, n), 0)
    si = lax.broadcasted_iota(jnp.int32, (c, n), 1) % c
    strict = si < ti
    incl = si <= ti

    def stack(y):
        return jnp.concatenate([jnp.where(lane_head == hh, y, 0.0) for hh in range(H_D)], axis=0).astype(BF16)

    def blockdiag(p):
        return jnp.where(row_blk == col_blk, jnp.concatenate([p] * H_D, axis=0), 0.0).astype(BF16)

    pre = []
    for r, lw, k, v, kk, a in chunks:
        cw = lw
        sh = 1
        while sh < c:
            cw = cw + jnp.where(row >= sh, pltpu.roll(cw, sh, 0), 0.0)
            sh *= 2
        tot = cw[c - 1:c]
        e_out = jnp.exp(-cw)
        e_fut = jnp.exp(tot - cw)
        beta = kk * a
        lhs = jnp.concatenate([-kk * jnp.exp(cw - lw), r * jnp.exp(cw)], axis=0).astype(BF16)
        rhs = jnp.concatenate([stack(beta * e_out), stack(k * e_out)], axis=0)
        fut = jnp.concatenate([beta * e_fut, k * e_fut], axis=0).astype(BF16)
        pre.append((lhs, rhs, fut, jnp.exp(tot), stack(v), v))
    amats = [lax.dot_general(lhs, rhs, _NT, preferred_element_type=F32) for lhs, rhs, *_ in pre]
    a_ab = [jnp.where(strict, am[:c, :n], 0.0) for am in amats]
    a_ak = [jnp.where(strict, am[:c, n:], 0.0) for am in amats]
    a_r = [jnp.concatenate([jnp.where(incl, am[c:, :n], 0.0), jnp.where(incl, am[c:, n:], 0.0)], axis=1)
           for am in amats]
    tinv = [(si == ti).astype(F32) + x for x in a_ab]
    pw = [_bdot(x, blockdiag(x)) for x in a_ab]
    m = 4
    while m < c:
        both = [_bdot(jnp.concatenate([p, t], axis=0), blockdiag(p)) for p, t in zip(pw, tinv)]
        pw = [b[:c] for b in both]
        tinv = [t + b[c:] for t, b in zip(tinv, both)]
        m *= 2
    tinv = [t + _bdot(t, blockdiag(p)) for p, t in zip(pw, tinv)]
    akv = [_bdot(x, p[4]) for x, p in zip(a_ak, pre)]
    outs = []
    for g, (lhs, _, fut, decay, vs, v) in enumerate(pre):
        xr = lax.dot_general(lhs, state.astype(BF16), _NT, preferred_element_type=F32)
        u = _bdot(tinv[g], stack(xr[:c] + akv[g]))
        outs.append(xr[c:] + _bdot(a_r[g], jnp.concatenate([stack(u), vs], axis=0)))
        upd = lax.dot_general(jnp.concatenate([u, v], axis=0).astype(BF16), fut, _TN, preferred_element_type=F32)
        state = jnp.where(row_head == lane_head, state * decay + upd, 0.0)
    return outs, state


def _wkv_prompt_body(zd_ref, sb_ref, mu_ref, w0_ref, w2_ref, a0_ref, a2_ref, g2_ref, kk_ref, ka_ref, rk_ref,
                     lnw_ref, lnb_ref, g64_ref, yd_ref, shift_ref, state_ref,
                     st_s, prev_s, r_s, lw_s, k_s, v_s, kk_s, a_s, o_s, *, tt, nt):
    t = pl.program_id(1)

    @pl.when(t == 0)
    def _():
        st_s[...] = jnp.zeros_like(st_s)
        prev_s[...] = sb_ref[0]

    x = zd_ref[0]
    first = lax.broadcasted_iota(jnp.int32, (tt, 1), 0) == 0
    prev = jnp.where(first, prev_s[...], pltpu.roll(x, 1, 0))
    last = x[tt - 1:tt]
    prev_s[...] = last
    g64 = g64_ref[...]
    r, lw, k2, v, kk, a, g = _rwkv_prep(x, prev, mu_ref[...], w0_ref[...], w2_ref[...], a0_ref[...], a2_ref[...],
                                        g2_ref[...], kk_ref[...], ka_ref[...], g64)
    r_s[...] = r
    lw_s[...] = lw
    k_s[...] = k2
    v_s[...] = v
    kk_s[...] = kk
    a_s[...] = a

    def chunk_group(c, _):
        rows = [pl.ds(pl.multiple_of((c * WKV_GROUP + u) * WKV_CHUNK, WKV_CHUNK), WKV_CHUNK)
                for u in range(WKV_GROUP)]
        outs, st = _wkv_chunks([(r_s[rw, :], lw_s[rw, :], k_s[rw, :], v_s[rw, :], kk_s[rw, :], a_s[rw, :])
                                for rw in rows], st_s[...])
        for rw, o in zip(rows, outs):
            o_s[rw, :] = o
        st_s[...] = st
        return 0

    lax.fori_loop(0, tt // (WKV_CHUNK * WKV_GROUP), chunk_group, 0)
    yd_ref[0] = _rwkv_post(o_s[...], r, k2, v, g, rk_ref[...], lnw_ref[...], lnb_ref[...], g64)

    @pl.when(t == nt - 1)
    def _():
        shift_ref[0] = last
        state_ref[0] = st_s[...]


def _rwkv_params(mu, w0, w2, a0, a2, g2, k_k, k_a, r_k, lnw, lnb):
    zeros = jnp.zeros((LORA_W, W_D), F32)
    row = lambda x: x.reshape(1, -1)
    return (row(mu), row(w0), jnp.concatenate([w2, zeros], axis=0).astype(BF16), row(a0),
            jnp.concatenate([zeros, a2], axis=0).astype(BF16), g2.astype(BF16), row(k_k), row(k_a), row(r_k),
            row(lnw), row(lnb), _group_ones(W_D, HEAD_DIM))


_RWKV_PARAM_SHAPES = ((1, N_COLS_D), (1, W_D), (LORA_W + LORA_A, W_D), (1, W_D), (LORA_W + LORA_A, W_D),
                      (LORA_G, W_D), (1, W_D), (1, W_D), (1, W_D), (1, W_D), (1, W_D), (W_D, W_D))


def _wkv_prompt(zd3, shift_buf, params, tt):
    nb, L, _ = zd3.shape
    nt = L // tt
    sc = lambda: pltpu.VMEM((tt, W_D), F32)
    return pl.pallas_call(
        functools.partial(_wkv_prompt_body, tt=tt, nt=nt),
        grid=(nb, nt),
        in_specs=[pl.BlockSpec((1, tt, N_COLS_D), lambda b, t: (b, t, 0)),
                  pl.BlockSpec((1, 1, N_COLS_D), lambda b, t: (b, 0, 0))]
                 + [_full(s) for s in _RWKV_PARAM_SHAPES],
        out_specs=[pl.BlockSpec((1, tt, W_D), lambda b, t: (b, t, 0)),
                   pl.BlockSpec((1, 1, N_COLS_D), lambda b, t: (b, 0, 0)),
                   pl.BlockSpec((1, W_D, W_D), lambda b, t: (b, 0, 0))],
        out_shape=[jax.ShapeDtypeStruct((nb, L, W_D), F32), jax.ShapeDtypeStruct((nb, 1, N_COLS_D), F32),
                   jax.ShapeDtypeStruct((nb, W_D, W_D), F32)],
        scratch_shapes=[pltpu.VMEM((W_D, W_D), F32), pltpu.VMEM((1, N_COLS_D), F32)] + [sc() for _ in range(7)],
        compiler_params=_cparams("parallel", "arbitrary"),
        name="wkv_prompt",
    )(zd3, shift_buf, *params)


def _wkv_decode_body(zd_ref, sb_ref, s0_ref, mu_ref, w0_ref, w2_ref, a0_ref, a2_ref, g2_ref, kk_ref, ka_ref, rk_ref,
                     lnw_ref, lnb_ref, g64_ref, yd_ref, shift_ref, state_ref, *, bb, L):
    x3 = zd_ref[...]
    tpos = lax.broadcasted_iota(jnp.int32, (1, L, 1), 1)
    prev3 = jnp.where(tpos == 0, sb_ref[...], pltpu.roll(x3, 1, 1))
    shift_ref[...] = x3[:, L - 1:L, :]
    g64 = g64_ref[...]
    flat = lambda y: y.reshape(bb * L, y.shape[-1])
    r, lw, k2, v, kk, a, g = _rwkv_prep(flat(x3), flat(prev3), mu_ref[...], w0_ref[...], w2_ref[...], a0_ref[...],
                                        a2_ref[...], g2_ref[...], kk_ref[...], ka_ref[...], g64)
    cube = lambda y: y.reshape(bb, L, W_D)
    r3, w3, k3, v3, kk3, b3 = cube(r), cube(jnp.exp(lw)), cube(k2), cube(v), cube(kk), cube(kk * a)
    eye = (lax.broadcasted_iota(jnp.int32, (HEAD_DIM, W_D), 0)
           == lax.broadcasted_iota(jnp.int32, (HEAD_DIM, W_D), 1) % HEAD_DIM).astype(F32)[None]

    def head_sum(y3):
        return _gsum(y3.reshape(bb * HEAD_DIM, W_D), g64).reshape(bb, HEAD_DIM, W_D)

    s = jnp.concatenate([s0_ref[:, hh] for hh in range(H_D)], axis=-1)
    outs = []
    for t in range(L):
        tok = lambda y: y[:, t:t + 1, :]
        sa = head_sum(s * -tok(kk3))
        vcol = head_sum(eye * tok(v3))
        s = s * tok(w3) + sa * tok(b3) + vcol * tok(k3)
        outs.append(jnp.sum(eye * head_sum(s * tok(r3)), axis=1, keepdims=True))
    for hh in range(H_D):
        state_ref[:, hh] = s[:, :, hh * HEAD_DIM:(hh + 1) * HEAD_DIM]
    o3 = jnp.concatenate(outs, axis=1)
    y = _rwkv_post(flat(o3), r, k2, v, g, rk_ref[...], lnw_ref[...], lnb_ref[...], g64)
    yd_ref[...] = y.reshape(bb, L, W_D)


def _wkv_decode(zd3, shift_all, s0_all, layer, params, bb):
    nb, L, _ = zd3.shape
    blk = lambda *s: pl.BlockSpec((bb,) + s, lambda i: (i,) + (0,) * len(s))
    lblk = lambda *s: pl.BlockSpec((None, bb) + s, lambda i: (layer, i) + (0,) * len(s))
    return pl.pallas_call(
        functools.partial(_wkv_decode_body, bb=bb, L=L),
        grid=(nb // bb,),
        in_specs=[blk(L, N_COLS_D), lblk(1, N_COLS_D), lblk(H_D, HEAD_DIM, HEAD_DIM)]
                 + [_full(s) for s in _RWKV_PARAM_SHAPES],
        out_specs=[blk(L, W_D), blk(1, N_COLS_D), blk(H_D, HEAD_DIM, HEAD_DIM)],
        out_shape=[jax.ShapeDtypeStruct((nb, L, W_D), F32), jax.ShapeDtypeStruct((nb, 1, N_COLS_D), F32),
                   jax.ShapeDtypeStruct((nb, H_D, HEAD_DIM, HEAD_DIM), F32)],
        compiler_params=_cparams("parallel"),
        name="wkv_decode",
    )(zd3, shift_all, s0_all, *params)


def _outproj_body(h_ref, ya_ref, yb_ref, yc_ref, yd_ref, w_ref, o_ref, *, yc_transposed):
    acc = h_ref[...]
    acc = acc + _bdot(ya_ref[...], w_ref[0:W_A, :])
    acc = acc + _bdot(yb_ref[...], w_ref[W_A:W_A + W_B, :])
    off = W_A + W_B
    yc = yc_ref[...].T if yc_transposed else yc_ref[...]
    acc = acc + _bdot(yc, w_ref[off:off + W_C, :])
    acc = acc + _bdot(yd_ref[...], w_ref[off + W_C:, :])
    o_ref[...] = acc


def _outproj(h, ya, yb, yc, yd, w_bf, tm, yc_transposed):
    t = h.shape[0]
    row = lambda n: pl.BlockSpec((tm, n), lambda i: (i, 0))
    ycs = pl.BlockSpec((W_C, tm), lambda i: (0, i)) if yc_transposed else row(W_C)
    return pl.pallas_call(
        functools.partial(_outproj_body, yc_transposed=yc_transposed),
        grid=(t // tm,),
        in_specs=[row(D_MODEL), row(W_A), row(W_B), ycs, row(W_D), _full((D_MODEL, D_MODEL))],
        out_specs=row(D_MODEL),
        out_shape=jax.ShapeDtypeStruct((t, D_MODEL), F32),
        compiler_params=_cparams("parallel"),
        name="outproj",
    )(h, ya, yb, yc, yd, w_bf)


def _cast_body(x_ref, o_ref):
    o_ref[...] = x_ref[...].astype(BF16)


def _to_bf16(x3, tr):
    e, r, c = x3.shape
    spec = pl.BlockSpec((1, tr, c), lambda i, j: (i, j, 0))
    return pl.pallas_call(
        _cast_body,
        grid=(e, r // tr),
        in_specs=[spec],
        out_specs=spec,
        out_shape=jax.ShapeDtypeStruct(x3.shape, BF16),
        compiler_params=_cparams("parallel", "parallel"),
        name="cast_bf16",
    )(x3)


def _ffn_body(blk_ref, h_ref, g_ref, w1_ref, w3_ref, w2_ref, o_ref, hn_s, acc_s, *, residual):
    del blk_ref
    j = pl.program_id(1)

    @pl.when(j == 0)
    def _():
        x = h_ref[...]
        hn_s[...] = (x * lax.rsqrt(jnp.mean(x * x, axis=-1, keepdims=True) + NORM_EPS) * g_ref[...]).astype(BF16)
        acc_s[...] = x if residual else jnp.zeros_like(x)

    hn = hn_s[...]
    a = jnp.dot(hn, w1_ref[0], preferred_element_type=F32)
    b = jnp.dot(hn, w3_ref[0], preferred_element_type=F32)
    acc_s[...] += _bdot(jax.nn.silu(a) * b, w2_ref[0])

    @pl.when(j == pl.num_programs(1) - 1)
    def _():
        o_ref[...] = acc_s[...]


def _ffn(blk_e, x, g, w1, w3, w2, tm, tf, residual):
    t = x.shape[0]
    ff = w1.shape[2]
    grid_spec = pltpu.PrefetchScalarGridSpec(
        num_scalar_prefetch=1,
        grid=(t // tm, ff // tf),
        in_specs=[pl.BlockSpec((tm, D_MODEL), lambda i, j, e: (i, 0)),
                  pl.BlockSpec((1, D_MODEL), lambda i, j, e: (0, 0)),
                  pl.BlockSpec((1, D_MODEL, tf), lambda i, j, e: (e[i], 0, j)),
                  pl.BlockSpec((1, D_MODEL, tf), lambda i, j, e: (e[i], 0, j)),
                  pl.BlockSpec((1, tf, D_MODEL), lambda i, j, e: (e[i], j, 0))],
        out_specs=pl.BlockSpec((tm, D_MODEL), lambda i, j, e: (i, 0)),
        scratch_shapes=[pltpu.VMEM((tm, D_MODEL), BF16), pltpu.VMEM((tm, D_MODEL), F32)],
    )
    return pl.pallas_call(
        functools.partial(_ffn_body, residual=residual),
        grid_spec=grid_spec,
        out_shape=jax.ShapeDtypeStruct((t, D_MODEL), F32),
        compiler_params=_cparams("parallel", "arbitrary"),
        name="ffn" if residual else "moe_ffn",
    )(blk_e, x, g.reshape(1, D_MODEL), w1, w3, w2)


def _router_body(h_ref, g_ref, w_ref, lg_ref):
    x = h_ref[...]
    hn = x * lax.rsqrt(jnp.mean(x * x, axis=-1, keepdims=True) + NORM_EPS) * g_ref[...]
    lg_ref[...] = _hdot(hn, w_ref[...])


def _router(h, g, w_pad, tm):
    t = h.shape[0]
    row = lambda n: pl.BlockSpec((tm, n), lambda i: (i, 0))
    return pl.pallas_call(
        _router_body,
        grid=(t // tm,),
        in_specs=[row(D_MODEL), _full((1, D_MODEL)), _full((D_MODEL, 128))],
        out_specs=row(128),
        out_shape=jax.ShapeDtypeStruct((t, 128), F32),
        compiler_params=_cparams("parallel"),
        name="router",
    )(h, g.reshape(1, D_MODEL), w_pad)


def _moe(h, g, router, w1, w3, w2, expert0, tm, tf):
    t = h.shape[0]
    logits = _router(h, g, jnp.pad(router, ((0, 0), (0, 128 - N_EXPERTS))), tm)
    top_logit, top_e = lax.top_k(logits[:, :N_EXPERTS], TOP_K)
    gates = jax.nn.softmax(top_logit, axis=-1)
    tk = t * TOP_K
    flat_e = top_e.reshape(tk)
    onehot = (flat_e[:, None] == jnp.arange(N_EXPERTS)[None, :]).astype(F32)
    seg = _row_tile(tk, RANK_SEG)
    oh3 = onehot.reshape(tk // seg, seg, N_EXPERTS)
    tril = jnp.tril(jnp.ones((seg, seg), F32))
    within = jnp.einsum('ts,bse->bte', tril, oh3, precision=lax.Precision.HIGHEST)
    seg_tot = within[:, -1, :]
    before = (within + (jnp.cumsum(seg_tot, axis=0) - seg_tot)[:, None, :]).reshape(tk, N_EXPERTS) - onehot
    rank = jnp.sum(onehot * before, axis=1).astype(jnp.int32)
    nblk = (jnp.sum(seg_tot, axis=0).astype(jnp.int32) + tm - 1) // tm
    blk_end = jnp.cumsum(nblk)
    dest = (blk_end - nblk)[flat_e] * tm + rank
    nb = -(-tk // tm) + N_EXPERTS
    row_tok = jnp.zeros((nb * tm,), jnp.int32).at[dest].set(jnp.arange(tk, dtype=jnp.int32) // TOP_K,
                                                            unique_indices=True)
    blk_e = jnp.minimum(jnp.searchsorted(blk_end, jnp.arange(nb), side='right'), N_EXPERTS - 1).astype(jnp.int32)
    xg = h[row_tok]
    yb = _ffn(blk_e + expert0, xg, g, w1, w3, w2, tm, tf, False)
    d2 = dest.reshape(t, TOP_K)
    return h + (yb[d2[:, 0]] * gates[:, 0:1] + yb[d2[:, 1]] * gates[:, 1:2])


def _ple_body(h_ref, g_ref, wg_ref, p_ref, wp_ref, o_ref):
    x = h_ref[...]
    hn = (x * lax.rsqrt(jnp.mean(x * x, axis=-1, keepdims=True) + NORM_EPS) * g_ref[...]).astype(BF16)
    gate = jax.nn.sigmoid(jnp.dot(hn, wg_ref[...], preferred_element_type=F32))
    o_ref[...] = x + gate * _bdot(p_ref[0], wp_ref[...])


def _ple(h, g, wg_bf, p_all, layer, wp_bf, tm):
    t = h.shape[0]
    row = lambda n: pl.BlockSpec((tm, n), lambda i: (i, 0))
    return pl.pallas_call(
        _ple_body,
        grid=(t // tm,),
        in_specs=[row(D_MODEL), _full((1, D_MODEL)), _full((D_MODEL, D_MODEL)),
                  pl.BlockSpec((1, tm, PLE_DIM), lambda i: (layer, i, 0)), _full((PLE_DIM, D_MODEL))],
        out_specs=row(D_MODEL),
        out_shape=jax.ShapeDtypeStruct((t, D_MODEL), F32),
        compiler_params=_cparams("parallel"),
        name="ple",
    )(h, g.reshape(1, D_MODEL), wg_bf, p_all, wp_bf)


def _row_tile(t, want):
    tm = min(t, want)
    assert t % tm == 0, (t, tm)
    return tm


def _run(x, p, decode, wts, cache, state_pool, state_shift, state_wkv, page_table):
    nb, L, _ = x.shape
    t = nb * L
    tm = _row_tile(t, 512)
    n_past = page_table.shape[1] * cache[0].shape[3] if decode else 0
    h = x.reshape(t, D_MODEL)
    ks, vs, pools, shifts, wkvs, chunk_vs = [], [], [], [], [], []
    causal = jnp.tril(jnp.ones((CHUNK, CHUNK), bool))
    for l in range(DEPTH):
        w = wts[l]
        za, zb, qn, kn, v, zd, qh, kh, vh = _inproj(h, w["norm_mix_g"], w["w_in"], w["c_qnorm_g"], w["c_knorm_g"], tm)

        ws = jnp.where(causal, w["a_ws"], 0.0)
        bs = w["a_bs"]
        if decode:
            reps = CHUNK // L
            ws = jax.vmap(lambda m: jnp.kron(jnp.eye(reps, dtype=F32), m[:L, :L]))(ws)
            bs = jnp.tile(bs[:, :L], (1, reps))
        wcat = jnp.concatenate([ws[hh] for hh in range(H_A)], axis=1).astype(BF16)
        bias = jnp.repeat(bs.T, HEAD_DIM, axis=1)
        ya, va = _mix_a(za, w["a_vnorm_g"], wcat, bias, _row_tile(t, 512) // CHUNK)

        if decode:
            buf16 = jnp.pad(state_pool[l], ((0, 0), (POOL_HALO - POOL_BUF, 0), (0, 0)))
        else:
            buf16 = jnp.zeros((nb, POOL_HALO, W_B), F32)
        proj_bd = jax.scipy.linalg.block_diag(*[w["b_proj"][gi] for gi in range(len(POOL_WINDOWS))]).astype(BF16)
        tl_b = _row_tile(L, 512)
        yb, pool16 = _mix_b(zb.reshape(nb, L, W_B), buf16, proj_bd, w["b_scale"], _row_tile(nb, 512 // tl_b), tl_b,
                            n_past)
        yb = yb.reshape(t, W_B)

        lam_init = 0.8 - 0.6 * math.exp(-0.3 * l)
        lp = w["c_lambda"]
        lam = (jnp.exp(jnp.sum(lp[0] * lp[1])) - jnp.exp(jnp.sum(lp[2] * lp[3])) + lam_init).reshape(1)
        if decode:
            yc = _attn_decode(page_table, lam, qn, kn, v, w["c_subln_g"], cache[0], cache[1], l, L, 1.0 - lam_init)
        else:
            tq = _row_tile(L, ATTN_TQ)
            yc = _attn_prompt(lam, qh, kh, vh, w["c_subln_g"], nb, L, tq, _row_tile(L, max(tq, ATTN_TK)), ATTN_NSUB,
                              1.0 - lam_init)

        params = _rwkv_params(w["d_mu"], w["d_w0"], w["d_w2"], w["d_a0"], w["d_a2"], w["d_g2"], w["d_kk"], w["d_ka"],
                              w["d_rk"].reshape(-1), w["d_lnx_w"], w["d_lnx_b"])
        zd3 = zd.reshape(nb, L, N_COLS_D)
        if decode:
            yd, shift_new, wkv_new = _wkv_decode(zd3, state_shift, state_wkv, l, params, _row_tile(nb, 16))
        else:
            yd, shift_new, st = _wkv_prompt(zd3, jnp.zeros((nb, 1, N_COLS_D), F32), params, _row_tile(L, 512))
            st = st.reshape(nb, H_D, HEAD_DIM, H_D, HEAD_DIM)
            wkv_new = jnp.stack([st[:, hh, :, hh, :] for hh in range(H_D)], axis=1)
        h = _outproj(h, ya, yb, yc, yd.reshape(t, W_D), w["w_out"], tm, yc_transposed=not decode)

        if l % 2 == 0:
            h = _ffn(jnp.full((t // tm,), w["expert0"], jnp.int32), h, w["norm_ffn_g"], w["ffn_w1"], w["ffn_w3"],
                     w["ffn_w2"], tm, D_FF // 2, True)
        else:
            h = _moe(h, w["norm_ffn_g"], w["moe_router"], w["moe_w1"], w["moe_w3"], w["moe_w2"], w["expert0"], tm,
                     D_FF // 2)

        h = _ple(h, w["ple_norm_g"], w["ple_gate"], p.reshape(DEPTH, t, PLE_DIM), l, w["ple_proj"], tm)

        ks.append(kn.reshape(nb, L, H_C, 2, DK_C))
        vs.append(v.reshape(nb, L, H_C, HEAD_DIM))
        pools.append(pool16[:, POOL_HALO - POOL_BUF:])
        shifts.append(shift_new)
        wkvs.append(wkv_new)
        chunk_vs.append(va.reshape(nb, L, W_A))
    return (h.reshape(nb, L, D_MODEL), jnp.stack(ks), jnp.stack(vs), jnp.stack(pools), jnp.stack(shifts),
            jnp.stack(wkvs), jnp.stack(chunk_vs))


def kernel(x_prompt, x_sample, cache_k, cache_v, state_pool, state_shift, state_wkv, page_table, p_prompt, p_sample, norm_mix_g, w_in, w_out, a_vnorm_g, a_ws, a_bs, b_proj, b_scale, c_qnorm_g, c_knorm_g, c_lambda, c_subln_g, d_mu, d_w0, d_w2, d_a0, d_a2, d_g2, d_kk, d_ka, d_rk, d_lnx_w, d_lnx_b, norm_ffn_g, ffn_w1, ffn_w3, ffn_w2, moe_router, moe_w1, moe_w3, moe_w2, ple_norm_g, ple_gate, ple_proj):
    bf = lambda a: a.astype(BF16)
    ffn_bf = (bf(ffn_w1), bf(ffn_w3), bf(ffn_w2))
    moe_bf = (_to_bf16(moe_w1.reshape(-1, D_MODEL, D_FF), 512), _to_bf16(moe_w3.reshape(-1, D_MODEL, D_FF), 512),
              _to_bf16(moe_w2.reshape(-1, D_FF, D_MODEL), D_FF // 4))
    wts = []
    for l in range(DEPTH):
        w = dict(norm_mix_g=norm_mix_g[l], w_in=bf(w_in[l]), w_out=bf(w_out[l]), a_vnorm_g=a_vnorm_g[l],
                 a_ws=a_ws[l], a_bs=a_bs[l], b_proj=b_proj[l], b_scale=b_scale[l], c_qnorm_g=c_qnorm_g[l],
                 c_knorm_g=c_knorm_g[l], c_lambda=c_lambda[l], c_subln_g=c_subln_g[l], d_mu=d_mu[l], d_w0=d_w0[l],
                 d_w2=d_w2[l], d_a0=d_a0[l], d_a2=d_a2[l], d_g2=d_g2[l], d_kk=d_kk[l], d_ka=d_ka[l], d_rk=d_rk[l],
                 d_lnx_w=d_lnx_w[l], d_lnx_b=d_lnx_b[l], norm_ffn_g=norm_ffn_g[l], ple_norm_g=ple_norm_g[l],
                 ple_gate=bf(ple_gate[l]), ple_proj=bf(ple_proj[l]))
        if l % 2 == 0:
            w.update(ffn_w1=ffn_bf[0], ffn_w3=ffn_bf[1], ffn_w2=ffn_bf[2], expert0=l // 2)
        else:
            w.update(moe_router=moe_router[l // 2], moe_w1=moe_bf[0], moe_w3=moe_bf[1], moe_w2=moe_bf[2],
                     expert0=N_EXPERTS * (l // 2))
        wts.append(w)
    depth, n_phys, page = cache_k.shape[:3]
    cache = (jnp.transpose(cache_k, (0, 1, 3, 4, 5, 2)).reshape(depth, n_phys, Q_C, page),
             jnp.transpose(cache_v, (0, 1, 3, 4, 2)).reshape(depth, n_phys, W_C, page))
    y_p, k_p, v_p, pool_p, shift_p, wkv_p, _ = _run(x_prompt, p_prompt, False, wts, cache, state_pool, state_shift,
                                                    state_wkv, page_table)
    y_s, k_s, v_s, pool_s, shift_s, wkv_s, cv_s = _run(x_sample, p_sample, True, wts, cache, state_pool, state_shift,
                                                       state_wkv, page_table)
    return (y_p, y_s, k_p, v_p, pool_p, shift_p, wkv_p, k_s, v_s, pool_s, shift_s, wkv_s, cv_s)
```

```python
import functools
import math

import jax
import jax.numpy as jnp
from jax import lax
from jax.experimental import pallas as pl
from jax.experimental.pallas import tpu as pltpu

F32 = jnp.float32
BF16 = jnp.bfloat16

D_MODEL = 1024
DEPTH = 4
HEAD_DIM = 64
W_A = W_B = W_C = W_D = 256
H_A = H_C = H_D = 4
CHUNK = 128
POOL_WINDOWS = (2, 4, 8, 16)
POOL_BUF = 15
POOL_HALO = 16
DK_C = 32
Q_C = 256
LORA_W = 64
LORA_A = 64
LORA_G = 128
N_COLS_D = 3 * W_D + LORA_W + LORA_A + LORA_G
OFF_A = 0
OFF_B = 512
OFF_C = 768
OFF_D = 1536
N_IN = 2560
D_FF = 2816
N_EXPERTS = 8
TOP_K = 2
PLE_DIM = 256
NORM_EPS = 1e-6
RWKV_GN_EPS = 64e-5
NEG_BIG = -1e30
LOG2_E = 1.4426950408889634

ATTN_TQ, ATTN_TK, ATTN_NSUB = 1024, 1024, 2
RANK_SEG = 512
ONES_ROWS = 16
WKV_GROUP = 8
WKV_CHUNK = 64
VMEM_LIMIT = 48 * 1024 * 1024

_NT = (((1,), (1,)), ((), ()))
_TN = (((0,), (0,)), ((), ()))


def _cparams(*sem):
    return pltpu.CompilerParams(dimension_semantics=sem, vmem_limit_bytes=VMEM_LIMIT)


def _bdot(a, b):
    return jnp.dot(a.astype(BF16), b.astype(BF16), preferred_element_type=F32)


def _hdot(a, b, dims=None):
    if dims is None:
        return jnp.dot(a, b, preferred_element_type=F32, precision=lax.Precision.HIGHEST)
    return lax.dot_general(a, b, dims, preferred_element_type=F32, precision=lax.Precision.HIGHEST)


def _gsum(x, gmat):
    hi = x.astype(BF16)
    lo = (x - hi.astype(F32)).astype(BF16)
    return (jnp.dot(hi, gmat, preferred_element_type=F32) + jnp.dot(lo, gmat, preferred_element_type=F32))


def _group_ones(n, g):
    i = jnp.arange(n) // g
    return (i[:, None] == i[None, :]).astype(BF16)


def _full(shape):
    nd = len(shape)
    return pl.BlockSpec(shape, lambda *_: (0,) * nd)


def _inproj_body(h_ref, g_ref, w_ref, gq_ref, gk_ref, g32_ref,
                 za_ref, zb_ref, qn_ref, kn_ref, v_ref, zd_ref, qh_ref, kh_ref, vt_ref):
    x = h_ref[...]
    xn = (x * lax.rsqrt(jnp.mean(x * x, axis=-1, keepdims=True) + NORM_EPS) * g_ref[...]).astype(BF16)

    def proj(lo, hi):
        return jnp.dot(xn, w_ref[:, lo:hi], preferred_element_type=F32)

    za_ref[...] = proj(OFF_A, OFF_B)
    zb_ref[...] = proj(OFF_B, OFF_C)
    zd_ref[...] = proj(OFF_D, N_IN)
    zq = proj(OFF_C, OFF_C + Q_C)
    zk = proj(OFF_C + Q_C, OFF_C + 2 * Q_C)
    zv = proj(OFF_C + 2 * Q_C, OFF_D)
    g32 = g32_ref[...]
    qn = zq * lax.rsqrt(_gsum(zq * zq, g32) * (1.0 / DK_C) + NORM_EPS) * gq_ref[...]
    kn = zk * lax.rsqrt(_gsum(zk * zk, g32) * (1.0 / DK_C) + NORM_EPS) * gk_ref[...]
    qn_ref[...] = qn
    kn_ref[...] = kn
    v_ref[...] = zv
    qs = qn * (DK_C ** -0.5 * LOG2_E)
    for hh in range(H_C):
        sl = slice(hh * HEAD_DIM, (hh + 1) * HEAD_DIM)
        qh_ref[hh] = qs[:, sl].astype(BF16)
        kh_ref[hh] = kn[:, sl].astype(BF16)
    vt_ref[...] = zv.T.astype(BF16)


def _inproj(h, g, w_bf, gq, gk, tm):
    t = h.shape[0]
    row = lambda n: pl.BlockSpec((tm, n), lambda i: (i, 0))
    headm = pl.BlockSpec((H_C, tm, HEAD_DIM), lambda i: (0, i, 0))
    f = lambda n: jax.ShapeDtypeStruct((t, n), F32)
    hm = jax.ShapeDtypeStruct((H_C, t, HEAD_DIM), BF16)
    return pl.pallas_call(
        _inproj_body,
        grid=(t // tm,),
        in_specs=[row(D_MODEL), _full((1, D_MODEL)), _full((D_MODEL, N_IN)), _full((1, Q_C)), _full((1, Q_C)),
                  _full((Q_C, Q_C))],
        out_specs=[row(2 * W_A), row(W_B), row(Q_C), row(Q_C), row(W_C), row(N_COLS_D), headm, headm,
                   pl.BlockSpec((W_C, tm), lambda i: (0, i))],
        out_shape=[f(2 * W_A), f(W_B), f(Q_C), f(Q_C), f(W_C), f(N_COLS_D), hm, hm,
                   jax.ShapeDtypeStruct((W_C, t), BF16)],
        compiler_params=_cparams("parallel"),
        name="inproj",
    )(h, g.reshape(1, D_MODEL), w_bf, jnp.tile(gq, 2 * H_C).reshape(1, Q_C), jnp.tile(gk, 2 * H_C).reshape(1, Q_C),
      _group_ones(Q_C, DK_C))


def _mix_a_body(z_ref, vg_ref, g64_ref, wcat_ref, bias_ref, ya_ref, va_ref, *, nchunk):
    z = z_ref[...]
    u = jax.nn.gelu(z[:, :W_A])
    vv = jax.nn.gelu(z[:, W_A:])
    vn = vv * lax.rsqrt(_gsum(vv * vv, g64_ref[...]) * (1.0 / HEAD_DIM) + NORM_EPS) * vg_ref[...]
    va_ref[...] = vn
    head = lax.broadcasted_iota(jnp.int32, (1, W_A), 1) // HEAD_DIM
    wcat = wcat_ref[...]
    bias = bias_ref[...]
    for c in range(nchunk):
        rows = slice(c * CHUNK, (c + 1) * CHUNK)
        vc = vn[rows]
        vstack = jnp.concatenate([jnp.where(head == hh, vc, 0.0) for hh in range(H_A)], axis=0).astype(BF16)
        s = jnp.dot(wcat, vstack, preferred_element_type=F32) + bias
        ya_ref[rows, :] = u[rows] * s


def _mix_a(za, vg, wcat_bf, bias, nchunk):
    t = za.shape[0]
    tm = nchunk * CHUNK
    return pl.pallas_call(
        functools.partial(_mix_a_body, nchunk=nchunk),
        grid=(t // tm,),
        in_specs=[pl.BlockSpec((tm, 2 * W_A), lambda i: (i, 0)), _full((1, W_A)), _full((W_A, W_A)),
                  _full((CHUNK, H_A * CHUNK)), _full((CHUNK, W_A))],
        out_specs=[pl.BlockSpec((tm, W_A), lambda i: (i, 0)), pl.BlockSpec((tm, W_A), lambda i: (i, 0))],
        out_shape=[jax.ShapeDtypeStruct((t, W_A), F32), jax.ShapeDtypeStruct((t, W_A), F32)],
        compiler_params=_cparams("parallel"),
        name="mix_a",
    )(za, vg.reshape(1, W_A), _group_ones(W_A, HEAD_DIM), wcat_bf, bias)


def _mix_b_body(z_ref, buf_ref, proj_ref, scale_ref, y_ref, pool_ref, halo_ref, *, bb, tl, start_pos, nt):
    t = pl.program_id(1)

    @pl.when(t == 0)
    def _():
        halo_ref[...] = buf_ref[...]

    z = z_ref[...]
    x = jnp.concatenate([halo_ref[...], z], axis=1)
    a2 = x + pltpu.roll(x, 1, 1)
    a4 = a2 + pltpu.roll(a2, 2, 1)
    a8 = a4 + pltpu.roll(a4, 4, 1)
    a16 = a8 + pltpu.roll(a8, 8, 1)
    grp = lax.broadcasted_iota(jnp.int32, (1, 1, W_B), 2) // (W_B // len(POOL_WINDOWS))
    win = jnp.where(grp == 0, a2, jnp.where(grp == 1, a4, jnp.where(grp == 2, a8, a16)))[:, POOL_HALO:]
    wsize = jnp.where(grp == 0, 2, jnp.where(grp == 1, 4, jnp.where(grp == 2, 8, 16)))
    pos = start_pos + t * tl + lax.broadcasted_iota(jnp.int32, (1, tl, 1), 1)
    cnt = jnp.minimum(wsize, pos + 1).astype(F32)
    pooled = (win / cnt - z).reshape(bb * tl, W_B)
    y_ref[...] = (_bdot(pooled, proj_ref[...]) * scale_ref[...]).reshape(bb, tl, W_B)
    tail = x[:, tl:]
    halo_ref[...] = tail

    @pl.when(t == nt - 1)
    def _():
        pool_ref[...] = tail


def _mix_b(zb3, buf16, proj_bd_bf, scale, bb, tl, start_pos):
    nb, L, _ = zb3.shape
    nt = L // tl
    return pl.pallas_call(
        functools.partial(_mix_b_body, bb=bb, tl=tl, start_pos=start_pos, nt=nt),
        grid=(nb // bb, nt),
        in_specs=[pl.BlockSpec((bb, tl, W_B), lambda b, t: (b, t, 0)),
                  pl.BlockSpec((bb, POOL_HALO, W_B), lambda b, t: (b, 0, 0)),
                  _full((W_B, W_B)), _full((1, W_B))],
        out_specs=[pl.BlockSpec((bb, tl, W_B), lambda b, t: (b, t, 0)),
                   pl.BlockSpec((bb, POOL_HALO, W_B), lambda b, t: (b, 0, 0))],
        out_shape=[jax.ShapeDtypeStruct((nb, L, W_B), F32), jax.ShapeDtypeStruct((nb, POOL_HALO, W_B), F32)],
        scratch_shapes=[pltpu.VMEM((bb, POOL_HALO, W_B), F32)],
        compiler_params=_cparams("parallel", "arbitrary"),
        name="mix_b",
    )(zb3, buf16, proj_bd_bf, scale.reshape(1, W_B))


def _attn_body(lam_ref, q_ref, k_ref, vt_ref, g_ref, o_ref, *, tq, tk, nsub, out_scale):
    qi = pl.program_id(2)
    q = q_ref[0]
    lane = lax.broadcasted_iota(jnp.int32, (1, HEAD_DIM), 1)
    zero = jnp.zeros_like(q)
    q0 = jnp.where(lane < DK_C, q, zero)
    q1 = jnp.where(lane >= DK_C, q, zero)

    ts = tk // nsub
    ones_rows = jnp.ones((ONES_ROWS, ts), BF16)

    def values(off):
        return jnp.concatenate([vt_ref[:, pl.ds(off, ts)], ones_rows], axis=0)

    def soft(s, vta, m, acc):
        mn = jnp.maximum(m, jnp.max(s, axis=0, keepdims=True))
        p = jnp.exp2(s - mn).astype(BF16)
        return mn, jnp.exp2(m - mn) * acc + jnp.dot(vta, p, preferred_element_type=F32)

    def block(j, carry, masked):
        m0, a0, m1, a1 = carry
        for u in range(nsub):
            off = pl.multiple_of(j * tk + u * ts, ts)
            kb = k_ref[0, pl.ds(off, ts), :]
            vta = values(off)
            s0 = lax.dot_general(kb, q0, _NT, preferred_element_type=F32)
            s1 = lax.dot_general(kb, q1, _NT, preferred_element_type=F32)
            if masked:
                key = off + lax.broadcasted_iota(jnp.int32, (ts, tq), 0)
                qry = qi * tq + lax.broadcasted_iota(jnp.int32, (ts, tq), 1)
                s0 = jnp.where(key <= qry, s0, NEG_BIG)
                s1 = jnp.where(key <= qry, s1, NEG_BIG)
            m0, a0 = soft(s0, vta, m0, a0)
            m1, a1 = soft(s1, vta, m1, a1)
        return m0, a0, m1, a1

    rowv = lambda v: jnp.full((1, tq), v, F32)
    acc = jnp.zeros((HEAD_DIM + ONES_ROWS, tq), F32)
    carry = (rowv(NEG_BIG), acc, rowv(NEG_BIG), acc)
    nfull = (qi * tq) // tk
    carry = lax.fori_loop(0, nfull, functools.partial(block, masked=False), carry)
    if tq == tk:
        tri = lax.broadcasted_iota(jnp.int32, (ts, ts), 0) <= lax.broadcasted_iota(jnp.int32, (ts, ts), 1)
        strips = []
        for h in range(nsub):
            cols = slice(h * ts, (h + 1) * ts)
            m0, a0, m1, a1 = (x[:, cols] for x in carry)
            for u in range(h + 1):
                off = pl.multiple_of(nfull * tk + u * ts, ts)
                kb = k_ref[0, pl.ds(off, ts), :]
                vta = values(off)
                s0 = lax.dot_general(kb, q0[cols], _NT, preferred_element_type=F32)
                s1 = lax.dot_general(kb, q1[cols], _NT, preferred_element_type=F32)
                if u == h:
                    s0 = jnp.where(tri, s0, NEG_BIG)
                    s1 = jnp.where(tri, s1, NEG_BIG)
                m0, a0 = soft(s0, vta, m0, a0)
                m1, a1 = soft(s1, vta, m1, a1)
            strips.append((a0, a1))
        a0, a1 = (jnp.concatenate(x, axis=1) for x in zip(*strips))
    else:
        _, a0, _, a1 = block(nfull, carry, True)
    l0, l1 = a0[HEAD_DIM:HEAD_DIM + 1], a1[HEAD_DIM:HEAD_DIM + 1]
    a0, a1 = a0[:HEAD_DIM], a1[:HEAD_DIM]
    o = a0 / l0 - lam_ref[0] * (a1 / l1)
    o_ref[...] = o * lax.rsqrt(jnp.mean(o * o, axis=0, keepdims=True) + NORM_EPS) * g_ref[...] * out_scale


def _attn_prompt(lam, qh, kh, vt, g, nb, seq, tq, tk, nsub, out_scale):
    nq = seq // tq
    assert tk % tq == 0 and seq % tk == 0 and tk % nsub == 0, (seq, tq, tk, nsub)
    qspec = pl.BlockSpec((1, tq, HEAD_DIM), lambda b, h, i: (h, b * nq + i, 0))
    kspec = pl.BlockSpec((1, seq, HEAD_DIM), lambda b, h, i: (h, b, 0))
    vspec = pl.BlockSpec((HEAD_DIM, seq), lambda b, h, i: (h, b))
    return pl.pallas_call(
        functools.partial(_attn_body, tq=tq, tk=tk, nsub=nsub, out_scale=out_scale),
        grid=(nb, H_C, nq),
        in_specs=[pl.BlockSpec(memory_space=pltpu.SMEM), qspec, kspec, vspec, _full((HEAD_DIM, 1))],
        out_specs=pl.BlockSpec((HEAD_DIM, tq), lambda b, h, i: (h, b * nq + i)),
        out_shape=jax.ShapeDtypeStruct((W_C, nb * seq), F32),
        compiler_params=_cparams("parallel", "parallel", "parallel"),
        name="attn_prompt",
    )(lam, qh, kh, vt, g.reshape(HEAD_DIM, 1))


def _attn_decode_body(pt_ref, lam_ref, q_ref, kn_ref, vn_ref, g_ref, g64_ref, *rest, npages, nq, out_scale):
    del pt_ref
    kp = rest[:npages]
    vp = rest[npages:2 * npages]
    o_ref = rest[2 * npages]
    nrow = 2 * H_C * nq
    q = q_ref[...] * (DK_C ** -0.5)
    row = lax.broadcasted_iota(jnp.int32, (nrow, 1), 0)
    lane = lax.broadcasted_iota(jnp.int32, (1, Q_C), 1)
    qs = jnp.concatenate([q] * (2 * H_C), axis=0)
    qs = jnp.where(row // nq == lane // DK_C, qs, 0.0).astype(BF16)
    s_pages = [_bdot(qs, kp[j][0, 0]) for j in range(npages)]
    s_new = lax.dot_general(qs, kn_ref[...].astype(BF16), _NT, preferred_element_type=F32)
    kidx = lax.broadcasted_iota(jnp.int32, (1, nq), 1)
    s_new = jnp.where(row % nq >= kidx, s_new, NEG_BIG)
    m = jnp.max(s_new, axis=-1, keepdims=True)
    for s in s_pages:
        m = jnp.maximum(m, jnp.max(s, axis=-1, keepdims=True))
    p_new = jnp.exp(s_new - m)
    l = jnp.sum(p_new, axis=-1, keepdims=True)
    p_pages = []
    for s in s_pages:
        p = jnp.exp(s - m)
        l = l + jnp.sum(p, axis=-1, keepdims=True)
        p_pages.append(p)
    coef = jnp.where((row // nq) % 2 == 0, 1.0, -lam_ref[0]) / l

    def diff(p):
        pw = p * coef
        return jnp.concatenate([pw[(2 * hh) * nq:(2 * hh + 1) * nq] + pw[(2 * hh + 1) * nq:(2 * hh + 2) * nq]
                                for hh in range(H_C)], axis=0)

    out = _bdot(diff(p_new), vn_ref[...])
    for j in range(npages):
        out = out + lax.dot_general(diff(p_pages[j]).astype(BF16), vp[j][0, 0].astype(BF16), _NT,
                                    preferred_element_type=F32)
    vlane = lax.broadcasted_iota(jnp.int32, (1, W_C), 1) // HEAD_DIM
    o = jnp.zeros((nq, W_C), F32)
    for hh in range(H_C):
        o = jnp.where(vlane == hh, out[hh * nq:(hh + 1) * nq], o)
    o = o * lax.rsqrt(_gsum(o * o, g64_ref[...]) * (1.0 / HEAD_DIM) + NORM_EPS) * g_ref[...] * out_scale
    o_ref[...] = o


def _attn_decode(page_table, lam, qn, kn, vn, g, cache_kt, cache_vt, layer, nq, out_scale):
    nb, npages = page_table.shape
    page = cache_kt.shape[3]
    row = pl.BlockSpec((nq, Q_C), lambda b, pt: (b, 0))

    def page_spec(j):
        return pl.BlockSpec((1, 1, Q_C, page), lambda b, pt, j=j: (layer, pt[b, j], 0, 0))

    grid_spec = pltpu.PrefetchScalarGridSpec(
        num_scalar_prefetch=1,
        grid=(nb,),
        in_specs=[pl.BlockSpec(memory_space=pltpu.SMEM), row, row, row,
                  pl.BlockSpec((1, W_C), lambda b, pt: (0, 0)), pl.BlockSpec((W_C, W_C), lambda b, pt: (0, 0))]
                 + [page_spec(j) for j in range(npages)] * 2,
        out_specs=row,
    )
    return pl.pallas_call(
        functools.partial(_attn_decode_body, npages=npages, nq=nq, out_scale=out_scale),
        grid_spec=grid_spec,
        out_shape=jax.ShapeDtypeStruct((nb * nq, W_C), F32),
        compiler_params=_cparams("parallel"),
        name="attn_decode",
    )(page_table, lam, qn, kn, vn, jnp.tile(g, H_C).reshape(1, W_C), _group_ones(W_C, HEAD_DIM),
      *([cache_kt] * npages), *([cache_vt] * npages))


def _rwkv_prep(x, prev, mu, w0, w2p, a0, a2p, g2, k_k, k_a, g64):
    xs = x + (prev - x) * mu
    r = xs[:, :W_D]
    k = xs[:, W_D:2 * W_D]
    v = xs[:, 2 * W_D:3 * W_D]
    hwa = xs[:, 3 * W_D:3 * W_D + LORA_W + LORA_A]
    hg = xs[:, 3 * W_D + LORA_W + LORA_A:]
    logdecay = -math.exp(-0.5) * jax.nn.sigmoid(w0 + _bdot(jnp.tanh(hwa), w2p))
    a = jax.nn.sigmoid(a0 + _bdot(hwa, a2p))
    g = _bdot(jax.nn.sigmoid(hg), g2)
    kk = k * k_k
    kk = kk / jnp.maximum(jnp.sqrt(_gsum(kk * kk, g64)), 1e-12)
    k2 = k * (1.0 + (a - 1.0) * k_a)
    return r, logdecay, k2, v, kk, a, g


def _rwkv_post(o, r, k2, v, g, rk, lnw, lnb, g64):
    inv_n = 1.0 / HEAD_DIM
    mean = _gsum(o, g64) * inv_n
    d = o - mean
    var = _gsum(d * d, g64) * inv_n
    on = d * lax.rsqrt(var + RWKV_GN_EPS) * lnw + lnb
    bonus = _gsum(r * k2 * rk, g64) * v
    return (on + bonus) * g


def _wkv_chunks(chunks, state):
    c = chunks[0][0].shape[0]
    n = H_D * c
    row = lax.broadcasted_iota(jnp.int32, (c, 1), 0)
    lane_head = lax.broadcasted_iota(jnp.int32, (1, W_D), 1) // HEAD_DIM
    row_head = lax.broadcasted_iota(jnp.int32, (W_D, 1), 0) // HEAD_DIM
    row_blk = lax.broadcasted_iota(jnp.int32, (n, 1), 0) // c
    col_blk = lax.broadcasted_iota(jnp.int32, (1, n), 1) // c
    ti = lax.broadcasted_iota(jnp.int32, (c, n), 0)
    si = lax.broadcasted_iota(jnp.int32, (c, n), 1) % c
    strict = si < ti
    incl = si <= ti

    def stack(y):
        return jnp.concatenate([jnp.where(lane_head == hh, y, 0.0) for hh in range(H_D)], axis=0).astype(BF16)

    def blockdiag(p):
        return jnp.where(row_blk == col_blk, jnp.concatenate([p] * H_D, axis=0), 0.0).astype(BF16)

    pre = []
    for r, lw, k, v, kk, a in chunks:
        cw = lw
        sh = 1
        while sh < c:
            cw = cw + jnp.where(row >= sh, pltpu.roll(cw, sh, 0), 0.0)
            sh *= 2
        tot = cw[c - 1:c]
        e_out = jnp.exp(-cw)
        e_fut = jnp.exp(tot - cw)
        beta = kk * a
        lhs = jnp.concatenate([-kk * jnp.exp(cw - lw), r * jnp.exp(cw)], axis=0).astype(BF16)
        rhs = jnp.concatenate([stack(beta * e_out), stack(k * e_out)], axis=0)
        fut = jnp.concatenate([beta * e_fut, k * e_fut], axis=0).astype(BF16)
        pre.append((lhs, rhs, fut, jnp.exp(tot), stack(v), v))
    amats = [lax.dot_general(lhs, rhs, _NT, preferred_element_type=F32) for lhs, rhs, *_ in pre]
    a_ab = [jnp.where(strict, am[:c, :n], 0.0) for am in amats]
    a_ak = [jnp.where(strict, am[:c, n:], 0.0) for am in amats]
    a_r = [jnp.concatenate([jnp.where(incl, am[c:, :n], 0.0), jnp.where(incl, am[c:, n:], 0.0)], axis=1)
           for am in amats]
    tinv = [(si == ti).astype(F32) + x for x in a_ab]
    pw = [_bdot(x, blockdiag(x)) for x in a_ab]
    m = 4
    while m < c:
        both = [_bdot(jnp.concatenate([p, t], axis=0), blockdiag(p)) for p, t in zip(pw, tinv)]
        pw = [b[:c] for b in both]
        tinv = [t + b[c:] for t, b in zip(tinv, both)]
        m *= 2
    tinv = [t + _bdot(t, blockdiag(p)) for p, t in zip(pw, tinv)]
    akv = [_bdot(x, p[4]) for x, p in zip(a_ak, pre)]
    outs = []
    for g, (lhs, _, fut, decay, vs, v) in enumerate(pre):
        xr = lax.dot_general(lhs, state.astype(BF16), _NT, preferred_element_type=F32)
        u = _bdot(tinv[g], stack(xr[:c] + akv[g]))
        outs.append(xr[c:] + _bdot(a_r[g], jnp.concatenate([stack(u), vs], axis=0)))
        upd = lax.dot_general(jnp.concatenate([u, v], axis=0).astype(BF16), fut, _TN, preferred_element_type=F32)
        state = jnp.where(row_head == lane_head, state * decay + upd, 0.0)
    return outs, state


def _wkv_prompt_body(zd_ref, sb_ref, mu_ref, w0_ref, w2_ref, a0_ref, a2_ref, g2_ref, kk_ref, ka_ref, rk_ref,
                     lnw_ref, lnb_ref, g64_ref, yd_ref, shift_ref, state_ref,
                     st_s, prev_s, r_s, lw_s, k_s, v_s, kk_s, a_s, o_s, *, tt, nt):
    t = pl.program_id(1)

    @pl.when(t == 0)
    def _():
        st_s[...] = jnp.zeros_like(st_s)
        prev_s[...] = sb_ref[0]

    x = zd_ref[0]
    first = lax.broadcasted_iota(jnp.int32, (tt, 1), 0) == 0
    prev = jnp.where(first, prev_s[...], pltpu.roll(x, 1, 0))
    last = x[tt - 1:tt]
    prev_s[...] = last
    g64 = g64_ref[...]
    r, lw, k2, v, kk, a, g = _rwkv_prep(x, prev, mu_ref[...], w0_ref[...], w2_ref[...], a0_ref[...], a2_ref[...],
                                        g2_ref[...], kk_ref[...], ka_ref[...], g64)
    r_s[...] = r
    lw_s[...] = lw
    k_s[...] = k2
    v_s[...] = v
    kk_s[...] = kk
    a_s[...] = a

    def chunk_group(c, _):
        rows = [pl.ds(pl.multiple_of((c * WKV_GROUP + u) * WKV_CHUNK, WKV_CHUNK), WKV_CHUNK)
                for u in range(WKV_GROUP)]
        outs, st = _wkv_chunks([(r_s[rw, :], lw_s[rw, :], k_s[rw, :], v_s[rw, :], kk_s[rw, :], a_s[rw, :])
                                for rw in rows], st_s[...])
        for rw, o in zip(rows, outs):
            o_s[rw, :] = o
        st_s[...] = st
        return 0

    lax.fori_loop(0, tt // (WKV_CHUNK * WKV_GROUP), chunk_group, 0)
    yd_ref[0] = _rwkv_post(o_s[...], r, k2, v, g, rk_ref[...], lnw_ref[...], lnb_ref[...], g64)

    @pl.when(t == nt - 1)
    def _():
        shift_ref[0] = last
        state_ref[0] = st_s[...]


def _rwkv_params(mu, w0, w2, a0, a2, g2, k_k, k_a, r_k, lnw, lnb):
    zeros = jnp.zeros((LORA_W, W_D), F32)
    row = lambda x: x.reshape(1, -1)
    return (row(mu), row(w0), jnp.concatenate([w2, zeros], axis=0).astype(BF16), row(a0),
            jnp.concatenate([zeros, a2], axis=0).astype(BF16), g2.astype(BF16), row(k_k), row(k_a), row(r_k),
            row(lnw), row(lnb), _group_ones(W_D, HEAD_DIM))


_RWKV_PARAM_SHAPES = ((1, N_COLS_D), (1, W_D), (LORA_W + LORA_A, W_D), (1, W_D), (LORA_W + LORA_A, W_D),
                      (LORA_G, W_D), (1, W_D), (1, W_D), (1, W_D), (1, W_D), (1, W_D), (W_D, W_D))


def _wkv_prompt(zd3, shift_buf, params, tt):
    nb, L, _ = zd3.shape
    nt = L // tt
    sc = lambda: pltpu.VMEM((tt, W_D), F32)
    return pl.pallas_call(
        functools.partial(_wkv_prompt_body, tt=tt, nt=nt),
        grid=(nb, nt),
        in_specs=[pl.BlockSpec((1, tt, N_COLS_D), lambda b, t: (b, t, 0)),
                  pl.BlockSpec((1, 1, N_COLS_D), lambda b, t: (b, 0, 0))]
                 + [_full(s) for s in _RWKV_PARAM_SHAPES],
        out_specs=[pl.BlockSpec((1, tt, W_D), lambda b, t: (b, t, 0)),
                   pl.BlockSpec((1, 1, N_COLS_D), lambda b, t: (b, 0, 0)),
                   pl.BlockSpec((1, W_D, W_D), lambda b, t: (b, 0, 0))],
        out_shape=[jax.ShapeDtypeStruct((nb, L, W_D), F32), jax.ShapeDtypeStruct((nb, 1, N_COLS_D), F32),
                   jax.ShapeDtypeStruct((nb, W_D, W_D), F32)],
        scratch_shapes=[pltpu.VMEM((W_D, W_D), F32), pltpu.VMEM((1, N_COLS_D), F32)] + [sc() for _ in range(7)],
        compiler_params=_cparams("parallel", "arbitrary"),
        name="wkv_prompt",
    )(zd3, shift_buf, *params)


def _wkv_decode_body(zd_ref, sb_ref, s0_ref, mu_ref, w0_ref, w2_ref, a0_ref, a2_ref, g2_ref, kk_ref, ka_ref, rk_ref,
                     lnw_ref, lnb_ref, g64_ref, yd_ref, shift_ref, state_ref, *, bb, L):
    x3 = zd_ref[...]
    tpos = lax.broadcasted_iota(jnp.int32, (1, L, 1), 1)
    prev3 = jnp.where(tpos == 0, sb_ref[...], pltpu.roll(x3, 1, 1))
    shift_ref[...] = x3[:, L - 1:L, :]
    g64 = g64_ref[...]
    flat = lambda y: y.reshape(bb * L, y.shape[-1])
    r, lw, k2, v, kk, a, g = _rwkv_prep(flat(x3), flat(prev3), mu_ref[...], w0_ref[...], w2_ref[...], a0_ref[...],
                                        a2_ref[...], g2_ref[...], kk_ref[...], ka_ref[...], g64)
    cube = lambda y: y.reshape(bb, L, W_D)
    r3, w3, k3, v3, kk3, b3 = cube(r), cube(jnp.exp(lw)), cube(k2), cube(v), cube(kk), cube(kk * a)
    eye = (lax.broadcasted_iota(jnp.int32, (HEAD_DIM, W_D), 0)
           == lax.broadcasted_iota(jnp.int32, (HEAD_DIM, W_D), 1) % HEAD_DIM).astype(F32)[None]

    def head_sum(y3):
        return _gsum(y3.reshape(bb * HEAD_DIM, W_D), g64).reshape(bb, HEAD_DIM, W_D)

    s = jnp.concatenate([s0_ref[:, hh] for hh in range(H_D)], axis=-1)
    outs = []
    for t in range(L):
        tok = lambda y: y[:, t:t + 1, :]
        sa = head_sum(s * -tok(kk3))
        vcol = head_sum(eye * tok(v3))
        s = s * tok(w3) + sa * tok(b3) + vcol * tok(k3)
        outs.append(jnp.sum(eye * head_sum(s * tok(r3)), axis=1, keepdims=True))
    for hh in range(H_D):
        state_ref[:, hh] = s[:, :, hh * HEAD_DIM:(hh + 1) * HEAD_DIM]
    o3 = jnp.concatenate(outs, axis=1)
    y = _rwkv_post(flat(o3), r, k2, v, g, rk_ref[...], lnw_ref[...], lnb_ref[...], g64)
    yd_ref[...] = y.reshape(bb, L, W_D)


def _wkv_decode(zd3, shift_all, s0_all, layer, params, bb):
    nb, L, _ = zd3.shape
    blk = lambda *s: pl.BlockSpec((bb,) + s, lambda i: (i,) + (0,) * len(s))
    lblk = lambda *s: pl.BlockSpec((None, bb) + s, lambda i: (layer, i) + (0,) * len(s))
    return pl.pallas_call(
        functools.partial(_wkv_decode_body, bb=bb, L=L),
        grid=(nb // bb,),
        in_specs=[blk(L, N_COLS_D), lblk(1, N_COLS_D), lblk(H_D, HEAD_DIM, HEAD_DIM)]
                 + [_full(s) for s in _RWKV_PARAM_SHAPES],
        out_specs=[blk(L, W_D), blk(1, N_COLS_D), blk(H_D, HEAD_DIM, HEAD_DIM)],
        out_shape=[jax.ShapeDtypeStruct((nb, L, W_D), F32), jax.ShapeDtypeStruct((nb, 1, N_COLS_D), F32),
                   jax.ShapeDtypeStruct((nb, H_D, HEAD_DIM, HEAD_DIM), F32)],
        compiler_params=_cparams("parallel"),
        name="wkv_decode",
    )(zd3, shift_all, s0_all, *params)


def _outproj_body(h_ref, ya_ref, yb_ref, yc_ref, yd_ref, w_ref, o_ref, *, yc_transposed):
    acc = h_ref[...]
    acc = acc + _bdot(ya_ref[...], w_ref[0:W_A, :])
    acc = acc + _bdot(yb_ref[...], w_ref[W_A:W_A + W_B, :])
    off = W_A + W_B
    yc = yc_ref[...].T if yc_transposed else yc_ref[...]
    acc = acc + _bdot(yc, w_ref[off:off + W_C, :])
    acc = acc + _bdot(yd_ref[...], w_ref[off + W_C:, :])
    o_ref[...] = acc


def _outproj(h, ya, yb, yc, yd, w_bf, tm, yc_transposed):
    t = h.shape[0]
    row = lambda n: pl.BlockSpec((tm, n), lambda i: (i, 0))
    ycs = pl.BlockSpec((W_C, tm), lambda i: (0, i)) if yc_transposed else row(W_C)
    return pl.pallas_call(
        functools.partial(_outproj_body, yc_transposed=yc_transposed),
        grid=(t // tm,),
        in_specs=[row(D_MODEL), row(W_A), row(W_B), ycs, row(W_D), _full((D_MODEL, D_MODEL))],
        out_specs=row(D_MODEL),
        out_shape=jax.ShapeDtypeStruct((t, D_MODEL), F32),
        compiler_params=_cparams("parallel"),
        name="outproj",
    )(h, ya, yb, yc, yd, w_bf)


def _cast_body(x_ref, o_ref):
    o_ref[...] = x_ref[...].astype(BF16)


def _to_bf16(x3, tr):
    e, r, c = x3.shape
    spec = pl.BlockSpec((1, tr, c), lambda i, j: (i, j, 0))
    return pl.pallas_call(
        _cast_body,
        grid=(e, r // tr),
        in_specs=[spec],
        out_specs=spec,
        out_shape=jax.ShapeDtypeStruct(x3.shape, BF16),
        compiler_params=_cparams("parallel", "parallel"),
        name="cast_bf16",
    )(x3)


def _ffn_body(blk_ref, h_ref, g_ref, w1_ref, w3_ref, w2_ref, o_ref, hn_s, acc_s, *, residual):
    del blk_ref
    j = pl.program_id(1)

    @pl.when(j == 0)
    def _():
        x = h_ref[...]
        hn_s[...] = (x * lax.rsqrt(jnp.mean(x * x, axis=-1, keepdims=True) + NORM_EPS) * g_ref[...]).astype(BF16)
        acc_s[...] = x if residual else jnp.zeros_like(x)

    hn = hn_s[...]
    a = jnp.dot(hn, w1_ref[0], preferred_element_type=F32)
    b = jnp.dot(hn, w3_ref[0], preferred_element_type=F32)
    acc_s[...] += _bdot(jax.nn.silu(a) * b, w2_ref[0])

    @pl.when(j == pl.num_programs(1) - 1)
    def _():
        o_ref[...] = acc_s[...]


def _ffn(blk_e, x, g, w1, w3, w2, tm, tf, residual):
    t = x.shape[0]
    ff = w1.shape[2]
    grid_spec = pltpu.PrefetchScalarGridSpec(
        num_scalar_prefetch=1,
        grid=(t // tm, ff // tf),
        in_specs=[pl.BlockSpec((tm, D_MODEL), lambda i, j, e: (i, 0)),
                  pl.BlockSpec((1, D_MODEL), lambda i, j, e: (0, 0)),
                  pl.BlockSpec((1, D_MODEL, tf), lambda i, j, e: (e[i], 0, j)),
                  pl.BlockSpec((1, D_MODEL, tf), lambda i, j, e: (e[i], 0, j)),
                  pl.BlockSpec((1, tf, D_MODEL), lambda i, j, e: (e[i], j, 0))],
        out_specs=pl.BlockSpec((tm, D_MODEL), lambda i, j, e: (i, 0)),
        scratch_shapes=[pltpu.VMEM((tm, D_MODEL), BF16), pltpu.VMEM((tm, D_MODEL), F32)],
    )
    return pl.pallas_call(
        functools.partial(_ffn_body, residual=residual),
        grid_spec=grid_spec,
        out_shape=jax.ShapeDtypeStruct((t, D_MODEL), F32),
        compiler_params=_cparams("parallel", "arbitrary"),
        name="ffn" if residual else "moe_ffn",
    )(blk_e, x, g.reshape(1, D_MODEL), w1, w3, w2)


def _router_body(h_ref, g_ref, w_ref, lg_ref):
    x = h_ref[...]
    hn = x * lax.rsqrt(jnp.mean(x * x, axis=-1, keepdims=True) + NORM_EPS) * g_ref[...]
    lg_ref[...] = _hdot(hn, w_ref[...])


def _router(h, g, w_pad, tm):
    t = h.shape[0]
    row = lambda n: pl.BlockSpec((tm, n), lambda i: (i, 0))
    return pl.pallas_call(
        _router_body,
        grid=(t // tm,),
        in_specs=[row(D_MODEL), _full((1, D_MODEL)), _full((D_MODEL, 128))],
        out_specs=row(128),
        out_shape=jax.ShapeDtypeStruct((t, 128), F32),
        compiler_params=_cparams("parallel"),
        name="router",
    )(h, g.reshape(1, D_MODEL), w_pad)


def _moe(h, g, router, w1, w3, w2, expert0, tm, tf):
    t = h.shape[0]
    logits = _router(h, g, jnp.pad(router, ((0, 0), (0, 128 - N_EXPERTS))), tm)
    top_logit, top_e = lax.top_k(logits[:, :N_EXPERTS], TOP_K)
    gates = jax.nn.softmax(top_logit, axis=-1)
    tk = t * TOP_K
    flat_e = top_e.reshape(tk)
    onehot = (flat_e[:, None] == jnp.arange(N_EXPERTS)[None, :]).astype(F32)
    seg = _row_tile(tk, RANK_SEG)
    oh3 = onehot.reshape(tk // seg, seg, N_EXPERTS)
    tril = jnp.tril(jnp.ones((seg, seg), F32))
    within = jnp.einsum('ts,bse->bte', tril, oh3, precision=lax.Precision.HIGHEST)
    seg_tot = within[:, -1, :]
    before = (within + (jnp.cumsum(seg_tot, axis=0) - seg_tot)[:, None, :]).reshape(tk, N_EXPERTS) - onehot
    rank = jnp.sum(onehot * before, axis=1).astype(jnp.int32)
    nblk = (jnp.sum(seg_tot, axis=0).astype(jnp.int32) + tm - 1) // tm
    blk_end = jnp.cumsum(nblk)
    dest = (blk_end - nblk)[flat_e] * tm + rank
    nb = -(-tk // tm) + N_EXPERTS
    row_tok = jnp.full((nb * tm,), t, jnp.int32).at[dest].set(jnp.arange(tk, dtype=jnp.int32) // TOP_K)
    blk_e = jnp.minimum(jnp.searchsorted(blk_end, jnp.arange(nb), side='right'), N_EXPERTS - 1).astype(jnp.int32)
    xg = jnp.concatenate([h, jnp.zeros((1, D_MODEL), F32)], axis=0)[row_tok]
    yb = _ffn(blk_e + expert0, xg, g, w1, w3, w2, tm, tf, False)
    d2 = dest.reshape(t, TOP_K)
    return h + (yb[d2[:, 0]] * gates[:, 0:1] + yb[d2[:, 1]] * gates[:, 1:2])


def _ple_body(h_ref, g_ref, wg_ref, p_ref, wp_ref, o_ref):
    x = h_ref[...]
    hn = (x * lax.rsqrt(jnp.mean(x * x, axis=-1, keepdims=True) + NORM_EPS) * g_ref[...]).astype(BF16)
    gate = jax.nn.sigmoid(jnp.dot(hn, wg_ref[...], preferred_element_type=F32))
    o_ref[...] = x + gate * _bdot(p_ref[0], wp_ref[...])


def _ple(h, g, wg_bf, p_all, layer, wp_bf, tm):
    t = h.shape[0]
    row = lambda n: pl.BlockSpec((tm, n), lambda i: (i, 0))
    return pl.pallas_call(
        _ple_body,
        grid=(t // tm,),
        in_specs=[row(D_MODEL), _full((1, D_MODEL)), _full((D_MODEL, D_MODEL)),
                  pl.BlockSpec((1, tm, PLE_DIM), lambda i: (layer, i, 0)), _full((PLE_DIM, D_MODEL))],
        out_specs=row(D_MODEL),
        out_shape=jax.ShapeDtypeStruct((t, D_MODEL), F32),
        compiler_params=_cparams("parallel"),
        name="ple",
    )(h, g.reshape(1, D_MODEL), wg_bf, p_all, wp_bf)


def _row_tile(t, want):
    tm = min(t, want)
    assert t % tm == 0, (t, tm)
    return tm


def _run(x, p, decode, wts, cache, state_pool, state_shift, state_wkv, page_table):
    nb, L, _ = x.shape
    t = nb * L
    tm = _row_tile(t, 512)
    n_past = page_table.shape[1] * cache[0].shape[3] if decode else 0
    h = x.reshape(t, D_MODEL)
    ks, vs, pools, shifts, wkvs, chunk_vs = [], [], [], [], [], []
    causal = jnp.tril(jnp.ones((CHUNK, CHUNK), bool))
    for l in range(DEPTH):
        w = wts[l]
        za, zb, qn, kn, v, zd, qh, kh, vh = _inproj(h, w["norm_mix_g"], w["w_in"], w["c_qnorm_g"], w["c_knorm_g"], tm)

        ws = jnp.where(causal, w["a_ws"], 0.0)
        bs = w["a_bs"]
        if decode:
            reps = CHUNK // L
            ws = jax.vmap(lambda m: jnp.kron(jnp.eye(reps, dtype=F32), m[:L, :L]))(ws)
            bs = jnp.tile(bs[:, :L], (1, reps))
        wcat = jnp.concatenate([ws[hh] for hh in range(H_A)], axis=1).astype(BF16)
        bias = jnp.repeat(bs.T, HEAD_DIM, axis=1)
        ya, va = _mix_a(za, w["a_vnorm_g"], wcat, bias, _row_tile(t, 512) // CHUNK)

        if decode:
            buf16 = jnp.pad(state_pool[l], ((0, 0), (POOL_HALO - POOL_BUF, 0), (0, 0)))
        else:
            buf16 = jnp.zeros((nb, POOL_HALO, W_B), F32)
        proj_bd = jax.scipy.linalg.block_diag(*[w["b_proj"][gi] for gi in range(len(POOL_WINDOWS))]).astype(BF16)
        tl_b = _row_tile(L, 512)
        yb, pool16 = _mix_b(zb.reshape(nb, L, W_B), buf16, proj_bd, w["b_scale"], _row_tile(nb, 512 // tl_b), tl_b,
                            n_past)
        yb = yb.reshape(t, W_B)

        lam_init = 0.8 - 0.6 * math.exp(-0.3 * l)
        lp = w["c_lambda"]
        lam = (jnp.exp(jnp.sum(lp[0] * lp[1])) - jnp.exp(jnp.sum(lp[2] * lp[3])) + lam_init).reshape(1)
        if decode:
            yc = _attn_decode(page_table, lam, qn, kn, v, w["c_subln_g"], cache[0], cache[1], l, L, 1.0 - lam_init)
        else:
            tq = _row_tile(L, ATTN_TQ)
            yc = _attn_prompt(lam, qh, kh, vh, w["c_subln_g"], nb, L, tq, _row_tile(L, max(tq, ATTN_TK)), ATTN_NSUB,
                              1.0 - lam_init)

        params = _rwkv_params(w["d_mu"], w["d_w0"], w["d_w2"], w["d_a0"], w["d_a2"], w["d_g2"], w["d_kk"], w["d_ka"],
                              w["d_rk"].reshape(-1), w["d_lnx_w"], w["d_lnx_b"])
        zd3 = zd.reshape(nb, L, N_COLS_D)
        if decode:
            yd, shift_new, wkv_new = _wkv_decode(zd3, state_shift, state_wkv, l, params, _row_tile(nb, 16))
        else:
            yd, shift_new, st = _wkv_prompt(zd3, jnp.zeros((nb, 1, N_COLS_D), F32), params, _row_tile(L, 512))
            st = st.reshape(nb, H_D, HEAD_DIM, H_D, HEAD_DIM)
            wkv_new = jnp.stack([st[:, hh, :, hh, :] for hh in range(H_D)], axis=1)
        h = _outproj(h, ya, yb, yc, yd.reshape(t, W_D), w["w_out"], tm, yc_transposed=not decode)

        if l % 2 == 0:
            h = _ffn(jnp.full((t // tm,), w["expert0"], jnp.int32), h, w["norm_ffn_g"], w["ffn_w1"], w["ffn_w3"],
                     w["ffn_w2"], tm, D_FF // 2, True)
        else:
            h = _moe(h, w["norm_ffn_g"], w["moe_router"], w["moe_w1"], w["moe_w3"], w["moe_w2"], w["expert0"], tm,
                     D_FF // 2)

        h = _ple(h, w["ple_norm_g"], w["ple_gate"], p.reshape(DEPTH, t, PLE_DIM), l, w["ple_proj"], tm)

        ks.append(kn.reshape(nb, L, H_C, 2, DK_C))
        vs.append(v.reshape(nb, L, H_C, HEAD_DIM))
        pools.append(pool16[:, POOL_HALO - POOL_BUF:])
        shifts.append(shift_new)
        wkvs.append(wkv_new)
        chunk_vs.append(va.reshape(nb, L, W_A))
    return (h.reshape(nb, L, D_MODEL), jnp.stack(ks), jnp.stack(vs), jnp.stack(pools), jnp.stack(shifts),
            jnp.stack(wkvs), jnp.stack(chunk_vs))


def kernel(x_prompt, x_sample, cache_k, cache_v, state_pool, state_shift, state_wkv, page_table, p_prompt, p_sample, norm_mix_g, w_in, w_out, a_vnorm_g, a_ws, a_bs, b_proj, b_scale, c_qnorm_g, c_knorm_g, c_lambda, c_subln_g, d_mu, d_w0, d_w2, d_a0, d_a2, d_g2, d_kk, d_ka, d_rk, d_lnx_w, d_lnx_b, norm_ffn_g, ffn_w1, ffn_w3, ffn_w2, moe_router, moe_w1, moe_w3, moe_w2, ple_norm_g, ple_gate, ple_proj):
    bf = lambda a: a.astype(BF16)
    ffn_bf = (bf(ffn_w1), bf(ffn_w3), bf(ffn_w2))
    moe_bf = (_to_bf16(moe_w1.reshape(-1, D_MODEL, D_FF), 512), _to_bf16(moe_w3.reshape(-1, D_MODEL, D_FF), 512),
              _to_bf16(moe_w2.reshape(-1, D_FF, D_MODEL), D_FF // 4))
    x_prompt, moe_bf = lax.optimization_barrier((x_prompt, moe_bf))
    wts = []
    for l in range(DEPTH):
        w = dict(norm_mix_g=norm_mix_g[l], w_in=bf(w_in[l]), w_out=bf(w_out[l]), a_vnorm_g=a_vnorm_g[l],
                 a_ws=a_ws[l], a_bs=a_bs[l], b_proj=b_proj[l], b_scale=b_scale[l], c_qnorm_g=c_qnorm_g[l],
                 c_knorm_g=c_knorm_g[l], c_lambda=c_lambda[l], c_subln_g=c_subln_g[l], d_mu=d_mu[l], d_w0=d_w0[l],
                 d_w2=d_w2[l], d_a0=d_a0[l], d_a2=d_a2[l], d_g2=d_g2[l], d_kk=d_kk[l], d_ka=d_ka[l], d_rk=d_rk[l],
                 d_lnx_w=d_lnx_w[l], d_lnx_b=d_lnx_b[l], norm_ffn_g=norm_ffn_g[l], ple_norm_g=ple_norm_g[l],
                 ple_gate=bf(ple_gate[l]), ple_proj=bf(ple_proj[l]))
        if l % 2 == 0:
            w.update(ffn_w1=ffn_bf[0], ffn_w3=ffn_bf[1], ffn_w2=ffn_bf[2], expert0=l // 2)
        else:
            w.update(moe_router=moe_router[l // 2], moe_w1=moe_bf[0], moe_w3=moe_bf[1], moe_w2=moe_bf[2],
                     expert0=N_EXPERTS * (l // 2))
        wts.append(w)
    depth, n_phys, page = cache_k.shape[:3]
    cache = (jnp.transpose(cache_k, (0, 1, 3, 4, 5, 2)).reshape(depth, n_phys, Q_C, page),
             jnp.transpose(cache_v, (0, 1, 3, 4, 2)).reshape(depth, n_phys, W_C, page))
    y_p, k_p, v_p, pool_p, shift_p, wkv_p, _ = _run(x_prompt, p_prompt, False, wts, cache, state_pool, state_shift,
                                                    state_wkv, page_table)
    y_s, k_s, v_s, pool_s, shift_s, wkv_s, cv_s = _run(x_sample, p_sample, True, wts, cache, state_pool, state_shift,
                                                       state_wkv, page_table)
    return (y_p, y_s, k_p, v_p, pool_p, shift_p, wkv_p, k_s, v_s, pool_s, shift_s, wkv_s, cv_s)
```

```python
import functools
import math

import jax
import jax.numpy as jnp
from jax import lax
from jax.experimental import pallas as pl
from jax.experimental.pallas import tpu as pltpu

F32 = jnp.float32
BF16 = jnp.bfloat16

D_MODEL = 1024
DEPTH = 4
HEAD_DIM = 64
W_A = W_B = W_C = W_D = 256
H_A = H_C = H_D = 4
CHUNK = 128
POOL_WINDOWS = (2, 4, 8, 16)
POOL_BUF = 15
POOL_HALO = 16
DK_C = 32
Q_C = 256
LORA_W = 64
LORA_A = 64
LORA_G = 128
N_COLS_D = 3 * W_D + LORA_W + LORA_A + LORA_G
OFF_A = 0
OFF_B = 512
OFF_C = 768
OFF_D = 1536
N_IN = 2560
D_FF = 2816
N_EXPERTS = 8
TOP_K = 2
PLE_DIM = 256
NORM_EPS = 1e-6
RWKV_GN_EPS = 64e-5
NEG_BIG = -1e30
LOG2_E = 1.4426950408889634

ATTN_TQ, ATTN_TK, ATTN_NSUB = 1024, 1024, 2
RANK_SEG = 512
ONES_ROWS = 16
WKV_GROUP = 8
WKV_CHUNK = 64
VMEM_LIMIT = 48 * 1024 * 1024

_NT = (((1,), (1,)), ((), ()))
_TN = (((0,), (0,)), ((), ()))


def _cparams(*sem):
    return pltpu.CompilerParams(dimension_semantics=sem, vmem_limit_bytes=VMEM_LIMIT)


def _bdot(a, b):
    return jnp.dot(a.astype(BF16), b.astype(BF16), preferred_element_type=F32)


def _hdot(a, b, dims=None):
    if dims is None:
        return jnp.dot(a, b, preferred_element_type=F32, precision=lax.Precision.HIGHEST)
    return lax.dot_general(a, b, dims, preferred_element_type=F32, precision=lax.Precision.HIGHEST)


def _gsum(x, gmat):
    hi = x.astype(BF16)
    lo = (x - hi.astype(F32)).astype(BF16)
    return (jnp.dot(hi, gmat, preferred_element_type=F32) + jnp.dot(lo, gmat, preferred_element_type=F32))


def _group_ones(n, g):
    i = jnp.arange(n) // g
    return (i[:, None] == i[None, :]).astype(BF16)


def _full(shape):
    nd = len(shape)
    return pl.BlockSpec(shape, lambda *_: (0,) * nd)


def _inproj_body(h_ref, g_ref, w_ref, gq_ref, gk_ref, g32_ref,
                 za_ref, zb_ref, qn_ref, kn_ref, v_ref, zd_ref, qh_ref, kh_ref, vt_ref):
    x = h_ref[...]
    xn = (x * lax.rsqrt(jnp.mean(x * x, axis=-1, keepdims=True) + NORM_EPS) * g_ref[...]).astype(BF16)

    def proj(lo, hi):
        return jnp.dot(xn, w_ref[:, lo:hi], preferred_element_type=F32)

    za_ref[...] = proj(OFF_A, OFF_B)
    zb_ref[...] = proj(OFF_B, OFF_C)
    zd_ref[...] = proj(OFF_D, N_IN)
    zq = proj(OFF_C, OFF_C + Q_C)
    zk = proj(OFF_C + Q_C, OFF_C + 2 * Q_C)
    zv = proj(OFF_C + 2 * Q_C, OFF_D)
    g32 = g32_ref[...]
    qn = zq * lax.rsqrt(_gsum(zq * zq, g32) * (1.0 / DK_C) + NORM_EPS) * gq_ref[...]
    kn = zk * lax.rsqrt(_gsum(zk * zk, g32) * (1.0 / DK_C) + NORM_EPS) * gk_ref[...]
    qn_ref[...] = qn
    kn_ref[...] = kn
    v_ref[...] = zv
    qs = qn * (DK_C ** -0.5 * LOG2_E)
    for hh in range(H_C):
        sl = slice(hh * HEAD_DIM, (hh + 1) * HEAD_DIM)
        qh_ref[hh] = qs[:, sl].astype(BF16)
        kh_ref[hh] = kn[:, sl].astype(BF16)
    vt_ref[...] = zv.T.astype(BF16)


def _inproj(h, g, w_bf, gq, gk, tm):
    t = h.shape[0]
    row = lambda n: pl.BlockSpec((tm, n), lambda i: (i, 0))
    headm = pl.BlockSpec((H_C, tm, HEAD_DIM), lambda i: (0, i, 0))
    f = lambda n: jax.ShapeDtypeStruct((t, n), F32)
    hm = jax.ShapeDtypeStruct((H_C, t, HEAD_DIM), BF16)
    return pl.pallas_call(
        _inproj_body,
        grid=(t // tm,),
        in_specs=[row(D_MODEL), _full((1, D_MODEL)), _full((D_MODEL, N_IN)), _full((1, Q_C)), _full((1, Q_C)),
                  _full((Q_C, Q_C))],
        out_specs=[row(2 * W_A), row(W_B), row(Q_C), row(Q_C), row(W_C), row(N_COLS_D), headm, headm,
                   pl.BlockSpec((W_C, tm), lambda i: (0, i))],
        out_shape=[f(2 * W_A), f(W_B), f(Q_C), f(Q_C), f(W_C), f(N_COLS_D), hm, hm,
                   jax.ShapeDtypeStruct((W_C, t), BF16)],
        compiler_params=_cparams("parallel"),
        name="inproj",
    )(h, g.reshape(1, D_MODEL), w_bf, jnp.tile(gq, 2 * H_C).reshape(1, Q_C), jnp.tile(gk, 2 * H_C).reshape(1, Q_C),
      _group_ones(Q_C, DK_C))


def _mix_a_body(z_ref, vg_ref, g64_ref, wcat_ref, bias_ref, ya_ref, va_ref, *, nchunk):
    z = z_ref[...]
    u = jax.nn.gelu(z[:, :W_A])
    vv = jax.nn.gelu(z[:, W_A:])
    vn = vv * lax.rsqrt(_gsum(vv * vv, g64_ref[...]) * (1.0 / HEAD_DIM) + NORM_EPS) * vg_ref[...]
    va_ref[...] = vn
    head = lax.broadcasted_iota(jnp.int32, (1, W_A), 1) // HEAD_DIM
    wcat = wcat_ref[...]
    bias = bias_ref[...]
    for c in range(nchunk):
        rows = slice(c * CHUNK, (c + 1) * CHUNK)
        vc = vn[rows]
        vstack = jnp.concatenate([jnp.where(head == hh, vc, 0.0) for hh in range(H_A)], axis=0).astype(BF16)
        s = jnp.dot(wcat, vstack, preferred_element_type=F32) + bias
        ya_ref[rows, :] = u[rows] * s


def _mix_a(za, vg, wcat_bf, bias, nchunk):
    t = za.shape[0]
    tm = nchunk * CHUNK
    return pl.pallas_call(
        functools.partial(_mix_a_body, nchunk=nchunk),
        grid=(t // tm,),
        in_specs=[pl.BlockSpec((tm, 2 * W_A), lambda i: (i, 0)), _full((1, W_A)), _full((W_A, W_A)),
                  _full((CHUNK, H_A * CHUNK)), _full((CHUNK, W_A))],
        out_specs=[pl.BlockSpec((tm, W_A), lambda i: (i, 0)), pl.BlockSpec((tm, W_A), lambda i: (i, 0))],
        out_shape=[jax.ShapeDtypeStruct((t, W_A), F32), jax.ShapeDtypeStruct((t, W_A), F32)],
        compiler_params=_cparams("parallel"),
        name="mix_a",
    )(za, vg.reshape(1, W_A), _group_ones(W_A, HEAD_DIM), wcat_bf, bias)


def _mix_b_body(z_ref, buf_ref, proj_ref, scale_ref, y_ref, pool_ref, halo_ref, *, bb, tl, start_pos, nt):
    t = pl.program_id(1)

    @pl.when(t == 0)
    def _():
        halo_ref[...] = buf_ref[...]

    z = z_ref[...]
    x = jnp.concatenate([halo_ref[...], z], axis=1)
    a2 = x + pltpu.roll(x, 1, 1)
    a4 = a2 + pltpu.roll(a2, 2, 1)
    a8 = a4 + pltpu.roll(a4, 4, 1)
    a16 = a8 + pltpu.roll(a8, 8, 1)
    grp = lax.broadcasted_iota(jnp.int32, (1, 1, W_B), 2) // (W_B // len(POOL_WINDOWS))
    win = jnp.where(grp == 0, a2, jnp.where(grp == 1, a4, jnp.where(grp == 2, a8, a16)))[:, POOL_HALO:]
    wsize = jnp.where(grp == 0, 2, jnp.where(grp == 1, 4, jnp.where(grp == 2, 8, 16)))
    pos = start_pos + t * tl + lax.broadcasted_iota(jnp.int32, (1, tl, 1), 1)
    cnt = jnp.minimum(wsize, pos + 1).astype(F32)
    pooled = (win / cnt - z).reshape(bb * tl, W_B)
    y_ref[...] = (_bdot(pooled, proj_ref[...]) * scale_ref[...]).reshape(bb, tl, W_B)
    tail = x[:, tl:]
    halo_ref[...] = tail

    @pl.when(t == nt - 1)
    def _():
        pool_ref[...] = tail


def _mix_b(zb3, buf16, proj_bd_bf, scale, bb, tl, start_pos):
    nb, L, _ = zb3.shape
    nt = L // tl
    return pl.pallas_call(
        functools.partial(_mix_b_body, bb=bb, tl=tl, start_pos=start_pos, nt=nt),
        grid=(nb // bb, nt),
        in_specs=[pl.BlockSpec((bb, tl, W_B), lambda b, t: (b, t, 0)),
                  pl.BlockSpec((bb, POOL_HALO, W_B), lambda b, t: (b, 0, 0)),
                  _full((W_B, W_B)), _full((1, W_B))],
        out_specs=[pl.BlockSpec((bb, tl, W_B), lambda b, t: (b, t, 0)),
                   pl.BlockSpec((bb, POOL_HALO, W_B), lambda b, t: (b, 0, 0))],
        out_shape=[jax.ShapeDtypeStruct((nb, L, W_B), F32), jax.ShapeDtypeStruct((nb, POOL_HALO, W_B), F32)],
        scratch_shapes=[pltpu.VMEM((bb, POOL_HALO, W_B), F32)],
        compiler_params=_cparams("parallel", "arbitrary"),
        name="mix_b",
    )(zb3, buf16, proj_bd_bf, scale.reshape(1, W_B))


def _attn_body(lam_ref, q_ref, k_ref, vt_ref, g_ref, o_ref, *, tq, tk, nsub, out_scale):
    qi = pl.program_id(2)
    q = q_ref[0]
    lane = lax.broadcasted_iota(jnp.int32, (1, HEAD_DIM), 1)
    zero = jnp.zeros_like(q)
    q0 = jnp.where(lane < DK_C, q, zero)
    q1 = jnp.where(lane >= DK_C, q, zero)

    ts = tk // nsub
    ones_rows = jnp.ones((ONES_ROWS, ts), BF16)

    def values(off):
        return jnp.concatenate([vt_ref[:, pl.ds(off, ts)], ones_rows], axis=0)

    def soft(s, vta, m, acc):
        mn = jnp.maximum(m, jnp.max(s, axis=0, keepdims=True))
        p = jnp.exp2(s - mn).astype(BF16)
        return mn, jnp.exp2(m - mn) * acc + jnp.dot(vta, p, preferred_element_type=F32)

    def block(j, carry, masked):
        m0, a0, m1, a1 = carry
        for u in range(nsub):
            off = pl.multiple_of(j * tk + u * ts, ts)
            kb = k_ref[0, pl.ds(off, ts), :]
            vta = values(off)
            s0 = lax.dot_general(kb, q0, _NT, preferred_element_type=F32)
            s1 = lax.dot_general(kb, q1, _NT, preferred_element_type=F32)
            if masked:
                key = off + lax.broadcasted_iota(jnp.int32, (ts, tq), 0)
                qry = qi * tq + lax.broadcasted_iota(jnp.int32, (ts, tq), 1)
                s0 = jnp.where(key <= qry, s0, NEG_BIG)
                s1 = jnp.where(key <= qry, s1, NEG_BIG)
            m0, a0 = soft(s0, vta, m0, a0)
            m1, a1 = soft(s1, vta, m1, a1)
        return m0, a0, m1, a1

    rowv = lambda v: jnp.full((1, tq), v, F32)
    acc = jnp.zeros((HEAD_DIM + ONES_ROWS, tq), F32)
    carry = (rowv(NEG_BIG), acc, rowv(NEG_BIG), acc)
    nfull = (qi * tq) // tk
    carry = lax.fori_loop(0, nfull, functools.partial(block, masked=False), carry)
    if tq == tk:
        tri = lax.broadcasted_iota(jnp.int32, (ts, ts), 0) <= lax.broadcasted_iota(jnp.int32, (ts, ts), 1)
        strips = []
        for h in range(nsub):
            cols = slice(h * ts, (h + 1) * ts)
            m0, a0, m1, a1 = (x[:, cols] for x in carry)
            for u in range(h + 1):
                off = pl.multiple_of(nfull * tk + u * ts, ts)
                kb = k_ref[0, pl.ds(off, ts), :]
                vta = values(off)
                s0 = lax.dot_general(kb, q0[cols], _NT, preferred_element_type=F32)
                s1 = lax.dot_general(kb, q1[cols], _NT, preferred_element_type=F32)
                if u == h:
                    s0 = jnp.where(tri, s0, NEG_BIG)
                    s1 = jnp.where(tri, s1, NEG_BIG)
                m0, a0 = soft(s0, vta, m0, a0)
                m1, a1 = soft(s1, vta, m1, a1)
            strips.append((a0, a1))
        a0, a1 = (jnp.concatenate(x, axis=1) for x in zip(*strips))
    else:
        _, a0, _, a1 = block(nfull, carry, True)
    l0, l1 = a0[HEAD_DIM:HEAD_DIM + 1], a1[HEAD_DIM:HEAD_DIM + 1]
    a0, a1 = a0[:HEAD_DIM], a1[:HEAD_DIM]
    o = a0 / l0 - lam_ref[0] * (a1 / l1)
    o_ref[...] = o * lax.rsqrt(jnp.mean(o * o, axis=0, keepdims=True) + NORM_EPS) * g_ref[...] * out_scale


def _attn_prompt(lam, qh, kh, vt, g, nb, seq, tq, tk, nsub, out_scale):
    nq = seq // tq
    assert tk % tq == 0 and seq % tk == 0 and tk % nsub == 0, (seq, tq, tk, nsub)
    qspec = pl.BlockSpec((1, tq, HEAD_DIM), lambda b, h, i: (h, b * nq + i, 0))
    kspec = pl.BlockSpec((1, seq, HEAD_DIM), lambda b, h, i: (h, b, 0))
    vspec = pl.BlockSpec((HEAD_DIM, seq), lambda b, h, i: (h, b))
    return pl.pallas_call(
        functools.partial(_attn_body, tq=tq, tk=tk, nsub=nsub, out_scale=out_scale),
        grid=(nb, H_C, nq),
        in_specs=[pl.BlockSpec(memory_space=pltpu.SMEM), qspec, kspec, vspec, _full((HEAD_DIM, 1))],
        out_specs=pl.BlockSpec((HEAD_DIM, tq), lambda b, h, i: (h, b * nq + i)),
        out_shape=jax.ShapeDtypeStruct((W_C, nb * seq), F32),
        compiler_params=_cparams("parallel", "parallel", "parallel"),
        name="attn_prompt",
    )(lam, qh, kh, vt, g.reshape(HEAD_DIM, 1))


def _attn_decode_body(pt_ref, lam_ref, q_ref, kn_ref, vn_ref, g_ref, g64_ref, *rest, npages, nq, out_scale):
    del pt_ref
    kp = rest[:npages]
    vp = rest[npages:2 * npages]
    o_ref = rest[2 * npages]
    nrow = 2 * H_C * nq
    q = q_ref[...] * (DK_C ** -0.5)
    row = lax.broadcasted_iota(jnp.int32, (nrow, 1), 0)
    lane = lax.broadcasted_iota(jnp.int32, (1, Q_C), 1)
    qs = jnp.concatenate([q] * (2 * H_C), axis=0)
    qs = jnp.where(row // nq == lane // DK_C, qs, 0.0).astype(BF16)
    s_pages = [_bdot(qs, kp[j][0, 0]) for j in range(npages)]
    s_new = lax.dot_general(qs, kn_ref[...].astype(BF16), _NT, preferred_element_type=F32)
    kidx = lax.broadcasted_iota(jnp.int32, (1, nq), 1)
    s_new = jnp.where(row % nq >= kidx, s_new, NEG_BIG)
    m = jnp.max(s_new, axis=-1, keepdims=True)
    for s in s_pages:
        m = jnp.maximum(m, jnp.max(s, axis=-1, keepdims=True))
    p_new = jnp.exp(s_new - m)
    l = jnp.sum(p_new, axis=-1, keepdims=True)
    p_pages = []
    for s in s_pages:
        p = jnp.exp(s - m)
        l = l + jnp.sum(p, axis=-1, keepdims=True)
        p_pages.append(p)
    coef = jnp.where((row // nq) % 2 == 0, 1.0, -lam_ref[0]) / l

    def diff(p):
        pw = p * coef
        return jnp.concatenate([pw[(2 * hh) * nq:(2 * hh + 1) * nq] + pw[(2 * hh + 1) * nq:(2 * hh + 2) * nq]
                                for hh in range(H_C)], axis=0)

    out = _bdot(diff(p_new), vn_ref[...])
    for j in range(npages):
        out = out + lax.dot_general(diff(p_pages[j]).astype(BF16), vp[j][0, 0].astype(BF16), _NT,
                                    preferred_element_type=F32)
    vlane = lax.broadcasted_iota(jnp.int32, (1, W_C), 1) // HEAD_DIM
    o = jnp.zeros((nq, W_C), F32)
    for hh in range(H_C):
        o = jnp.where(vlane == hh, out[hh * nq:(hh + 1) * nq], o)
    o = o * lax.rsqrt(_gsum(o * o, g64_ref[...]) * (1.0 / HEAD_DIM) + NORM_EPS) * g_ref[...] * out_scale
    o_ref[...] = o


def _attn_decode(page_table, lam, qn, kn, vn, g, cache_kt, cache_vt, layer, nq, out_scale):
    nb, npages = page_table.shape
    page = cache_kt.shape[3]
    row = pl.BlockSpec((nq, Q_C), lambda b, pt: (b, 0))

    def page_spec(j):
        return pl.BlockSpec((1, 1, Q_C, page), lambda b, pt, j=j: (layer, pt[b, j], 0, 0))

    grid_spec = pltpu.PrefetchScalarGridSpec(
        num_scalar_prefetch=1,
        grid=(nb,),
        in_specs=[pl.BlockSpec(memory_space=pltpu.SMEM), row, row, row,
                  pl.BlockSpec((1, W_C), lambda b, pt: (0, 0)), pl.BlockSpec((W_C, W_C), lambda b, pt: (0, 0))]
                 + [page_spec(j) for j in range(npages)] * 2,
        out_specs=row,
    )
    return pl.pallas_call(
        functools.partial(_attn_decode_body, npages=npages, nq=nq, out_scale=out_scale),
        grid_spec=grid_spec,
        out_shape=jax.ShapeDtypeStruct((nb * nq, W_C), F32),
        compiler_params=_cparams("parallel"),
        name="attn_decode",
    )(page_table, lam, qn, kn, vn, jnp.tile(g, H_C).reshape(1, W_C), _group_ones(W_C, HEAD_DIM),
      *([cache_kt] * npages), *([cache_vt] * npages))


def _rwkv_prep(x, prev, mu, w0, w2p, a0, a2p, g2, k_k, k_a, g64):
    xs = x + (prev - x) * mu
    r = xs[:, :W_D]
    k = xs[:, W_D:2 * W_D]
    v = xs[:, 2 * W_D:3 * W_D]
    hwa = xs[:, 3 * W_D:3 * W_D + LORA_W + LORA_A]
    hg = xs[:, 3 * W_D + LORA_W + LORA_A:]
    logdecay = -math.exp(-0.5) * jax.nn.sigmoid(w0 + _bdot(jnp.tanh(hwa), w2p))
    a = jax.nn.sigmoid(a0 + _bdot(hwa, a2p))
    g = _bdot(jax.nn.sigmoid(hg), g2)
    kk = k * k_k
    kk = kk / jnp.maximum(jnp.sqrt(_gsum(kk * kk, g64)), 1e-12)
    k2 = k * (1.0 + (a - 1.0) * k_a)
    return r, logdecay, k2, v, kk, a, g


def _rwkv_post(o, r, k2, v, g, rk, lnw, lnb, g64):
    inv_n = 1.0 / HEAD_DIM
    mean = _gsum(o, g64) * inv_n
    d = o - mean
    var = _gsum(d * d, g64) * inv_n
    on = d * lax.rsqrt(var + RWKV_GN_EPS) * lnw + lnb
    bonus = _gsum(r * k2 * rk, g64) * v
    return (on + bonus) * g


def _wkv_chunks(chunks, state):
    c = chunks[0][0].shape[0]
    n = H_D * c
    row = lax.broadcasted_iota(jnp.int32, (c, 1), 0)
    lane_head = lax.broadcasted_iota(jnp.int32, (1, W_D), 1) // HEAD_DIM
    row_head = lax.broadcasted_iota(jnp.int32, (W_D, 1), 0) // HEAD_DIM
    row_blk = lax.broadcasted_iota(jnp.int32, (n, 1), 0) // c
    col_blk = lax.broadcasted_iota(jnp.int32, (1, n), 1) // c
    ti = lax.broadcasted_iota(jnp.int32, (c, n), 0)
    si = lax.broadcasted_iota(jnp.int32, (c, n), 1) % c
    strict = si < ti
    incl = si <= ti

    def stack(y):
        return jnp.concatenate([jnp.where(lane_head == hh, y, 0.0) for hh in range(H_D)], axis=0).astype(BF16)

    def blockdiag(p):
        return jnp.where(row_blk == col_blk, jnp.concatenate([p] * H_D, axis=0), 0.0).astype(BF16)

    pre = []
    for r, lw, k, v, kk, a in chunks:
        cw = lw
        sh = 1
        while sh < c:
            cw = cw + jnp.where(row >= sh, pltpu.roll(cw, sh, 0), 0.0)
            sh *= 2
        tot = cw[c - 1:c]
        e_out = jnp.exp(-cw)
        e_fut = jnp.exp(tot - cw)
        beta = kk * a
        lhs = jnp.concatenate([-kk * jnp.exp(cw - lw), r * jnp.exp(cw)], axis=0).astype(BF16)
        rhs = jnp.concatenate([stack(beta * e_out), stack(k * e_out)], axis=0)
        fut = jnp.concatenate([beta * e_fut, k * e_fut], axis=0).astype(BF16)
        pre.append((lhs, rhs, fut, jnp.exp(tot), stack(v), v))
    amats = [lax.dot_general(lhs, rhs, _NT, preferred_element_type=F32) for lhs, rhs, *_ in pre]
    a_ab = [jnp.where(strict, am[:c, :n], 0.0) for am in amats]
    a_ak = [jnp.where(strict, am[:c, n:], 0.0) for am in amats]
    a_r = [jnp.concatenate([jnp.where(incl, am[c:, :n], 0.0), jnp.where(incl, am[c:, n:], 0.0)], axis=1)
           for am in amats]
    tinv = [(si == ti).astype(F32) + x for x in a_ab]
    pw = [_bdot(x, blockdiag(x)) for x in a_ab]
    m = 4
    while m < c:
        both = [_bdot(jnp.concatenate([p, t], axis=0), blockdiag(p)) for p, t in zip(pw, tinv)]
        pw = [b[:c] for b in both]
        tinv = [t + b[c:] for t, b in zip(tinv, both)]
        m *= 2
    tinv = [t + _bdot(t, blockdiag(p)) for p, t in zip(pw, tinv)]
    akv = [_bdot(x, p[4]) for x, p in zip(a_ak, pre)]
    outs = []
    for g, (lhs, _, fut, decay, vs, v) in enumerate(pre):
        xr = lax.dot_general(lhs, state.astype(BF16), _NT, preferred_element_type=F32)
        u = _bdot(tinv[g], stack(xr[:c] + akv[g]))
        outs.append(xr[c:] + _bdot(a_r[g], jnp.concatenate([stack(u), vs], axis=0)))
        upd = lax.dot_general(jnp.concatenate([u, v], axis=0).astype(BF16), fut, _TN, preferred_element_type=F32)
        state = jnp.where(row_head == lane_head, state * decay + upd, 0.0)
    return outs, state


def _wkv_prompt_body(zd_ref, sb_ref, mu_ref, w0_ref, w2_ref, a0_ref, a2_ref, g2_ref, kk_ref, ka_ref, rk_ref,
                     lnw_ref, lnb_ref, g64_ref, yd_ref, shift_ref, state_ref,
                     st_s, prev_s, r_s, lw_s, k_s, v_s, kk_s, a_s, o_s, *, tt, nt):
    t = pl.program_id(1)

    @pl.when(t == 0)
    def _():
        st_s[...] = jnp.zeros_like(st_s)
        prev_s[...] = sb_ref[0]

    x = zd_ref[0]
    first = lax.broadcasted_iota(jnp.int32, (tt, 1), 0) == 0
    prev = jnp.where(first, prev_s[...], pltpu.roll(x, 1, 0))
    last = x[tt - 1:tt]
    prev_s[...] = last
    g64 = g64_ref[...]
    r, lw, k2, v, kk, a, g = _rwkv_prep(x, prev, mu_ref[...], w0_ref[...], w2_ref[...], a0_ref[...], a2_ref[...],
                                        g2_ref[...], kk_ref[...], ka_ref[...], g64)
    r_s[...] = r
    lw_s[...] = lw
    k_s[...] = k2
    v_s[...] = v
    kk_s[...] = kk
    a_s[...] = a

    def chunk_group(c, _):
        rows = [pl.ds(pl.multiple_of((c * WKV_GROUP + u) * WKV_CHUNK, WKV_CHUNK), WKV_CHUNK)
                for u in range(WKV_GROUP)]
        outs, st = _wkv_chunks([(r_s[rw, :], lw_s[rw, :], k_s[rw, :], v_s[rw, :], kk_s[rw, :], a_s[rw, :])
                                for rw in rows], st_s[...])
        for rw, o in zip(rows, outs):
            o_s[rw, :] = o
        st_s[...] = st
        return 0

    lax.fori_loop(0, tt // (WKV_CHUNK * WKV_GROUP), chunk_group, 0)
    yd_ref[0] = _rwkv_post(o_s[...], r, k2, v, g, rk_ref[...], lnw_ref[...], lnb_ref[...], g64)

    @pl.when(t == nt - 1)
    def _():
        shift_ref[0] = last
        state_ref[0] = st_s[...]


def _rwkv_params(mu, w0, w2, a0, a2, g2, k_k, k_a, r_k, lnw, lnb):
    zeros = jnp.zeros((LORA_W, W_D), F32)
    row = lambda x: x.reshape(1, -1)
    return (row(mu), row(w0), jnp.concatenate([w2, zeros], axis=0).astype(BF16), row(a0),
            jnp.concatenate([zeros, a2], axis=0).astype(BF16), g2.astype(BF16), row(k_k), row(k_a), row(r_k),
            row(lnw), row(lnb), _group_ones(W_D, HEAD_DIM))


_RWKV_PARAM_SHAPES = ((1, N_COLS_D), (1, W_D), (LORA_W + LORA_A, W_D), (1, W_D), (LORA_W + LORA_A, W_D),
                      (LORA_G, W_D), (1, W_D), (1, W_D), (1, W_D), (1, W_D), (1, W_D), (W_D, W_D))


def _wkv_prompt(zd3, shift_buf, params, tt):
    nb, L, _ = zd3.shape
    nt = L // tt
    sc = lambda: pltpu.VMEM((tt, W_D), F32)
    return pl.pallas_call(
        functools.partial(_wkv_prompt_body, tt=tt, nt=nt),
        grid=(nb, nt),
        in_specs=[pl.BlockSpec((1, tt, N_COLS_D), lambda b, t: (b, t, 0)),
                  pl.BlockSpec((1, 1, N_COLS_D), lambda b, t: (b, 0, 0))]
                 + [_full(s) for s in _RWKV_PARAM_SHAPES],
        out_specs=[pl.BlockSpec((1, tt, W_D), lambda b, t: (b, t, 0)),
                   pl.BlockSpec((1, 1, N_COLS_D), lambda b, t: (b, 0, 0)),
                   pl.BlockSpec((1, W_D, W_D), lambda b, t: (b, 0, 0))],
        out_shape=[jax.ShapeDtypeStruct((nb, L, W_D), F32), jax.ShapeDtypeStruct((nb, 1, N_COLS_D), F32),
                   jax.ShapeDtypeStruct((nb, W_D, W_D), F32)],
        scratch_shapes=[pltpu.VMEM((W_D, W_D), F32), pltpu.VMEM((1, N_COLS_D), F32)] + [sc() for _ in range(7)],
        compiler_params=_cparams("parallel", "arbitrary"),
        name="wkv_prompt",
    )(zd3, shift_buf, *params)


def _wkv_decode_body(zd_ref, sb_ref, s0_ref, mu_ref, w0_ref, w2_ref, a0_ref, a2_ref, g2_ref, kk_ref, ka_ref, rk_ref,
                     lnw_ref, lnb_ref, g64_ref, yd_ref, shift_ref, state_ref, *, bb, L):
    x3 = zd_ref[...]
    tpos = lax.broadcasted_iota(jnp.int32, (1, L, 1), 1)
    prev3 = jnp.where(tpos == 0, sb_ref[...], pltpu.roll(x3, 1, 1))
    shift_ref[...] = x3[:, L - 1:L, :]
    g64 = g64_ref[...]
    flat = lambda y: y.reshape(bb * L, y.shape[-1])
    r, lw, k2, v, kk, a, g = _rwkv_prep(flat(x3), flat(prev3), mu_ref[...], w0_ref[...], w2_ref[...], a0_ref[...],
                                        a2_ref[...], g2_ref[...], kk_ref[...], ka_ref[...], g64)
    cube = lambda y: y.reshape(bb, L, W_D)
    r3, w3, k3, v3, kk3, b3 = cube(r), cube(jnp.exp(lw)), cube(k2), cube(v), cube(kk), cube(kk * a)
    eye = (lax.broadcasted_iota(jnp.int32, (HEAD_DIM, W_D), 0)
           == lax.broadcasted_iota(jnp.int32, (HEAD_DIM, W_D), 1) % HEAD_DIM).astype(F32)[None]

    def head_sum(y3):
        return _gsum(y3.reshape(bb * HEAD_DIM, W_D), g64).reshape(bb, HEAD_DIM, W_D)

    s = jnp.concatenate([s0_ref[:, hh] for hh in range(H_D)], axis=-1)
    outs = []
    for t in range(L):
        tok = lambda y: y[:, t:t + 1, :]
        sa = head_sum(s * -tok(kk3))
        vcol = head_sum(eye * tok(v3))
        s = s * tok(w3) + sa * tok(b3) + vcol * tok(k3)
        outs.append(jnp.sum(eye * head_sum(s * tok(r3)), axis=1, keepdims=True))
    for hh in range(H_D):
        state_ref[:, hh] = s[:, :, hh * HEAD_DIM:(hh + 1) * HEAD_DIM]
    o3 = jnp.concatenate(outs, axis=1)
    y = _rwkv_post(flat(o3), r, k2, v, g, rk_ref[...], lnw_ref[...], lnb_ref[...], g64)
    yd_ref[...] = y.reshape(bb, L, W_D)


def _wkv_decode(zd3, shift_all, s0_all, layer, params, bb):
    nb, L, _ = zd3.shape
    blk = lambda *s: pl.BlockSpec((bb,) + s, lambda i: (i,) + (0,) * len(s))
    lblk = lambda *s: pl.BlockSpec((None, bb) + s, lambda i: (layer, i) + (0,) * len(s))
    return pl.pallas_call(
        functools.partial(_wkv_decode_body, bb=bb, L=L),
        grid=(nb // bb,),
        in_specs=[blk(L, N_COLS_D), lblk(1, N_COLS_D), lblk(H_D, HEAD_DIM, HEAD_DIM)]
                 + [_full(s) for s in _RWKV_PARAM_SHAPES],
        out_specs=[blk(L, W_D), blk(1, N_COLS_D), blk(H_D, HEAD_DIM, HEAD_DIM)],
        out_shape=[jax.ShapeDtypeStruct((nb, L, W_D), F32), jax.ShapeDtypeStruct((nb, 1, N_COLS_D), F32),
                   jax.ShapeDtypeStruct((nb, H_D, HEAD_DIM, HEAD_DIM), F32)],
        compiler_params=_cparams("parallel"),
        name="wkv_decode",
    )(zd3, shift_all, s0_all, *params)


def _outproj_body(h_ref, ya_ref, yb_ref, yc_ref, yd_ref, w_ref, o_ref, *, yc_transposed):
    acc = h_ref[...]
    acc = acc + _bdot(ya_ref[...], w_ref[0:W_A, :])
    acc = acc + _bdot(yb_ref[...], w_ref[W_A:W_A + W_B, :])
    off = W_A + W_B
    yc = yc_ref[...].T if yc_transposed else yc_ref[...]
    acc = acc + _bdot(yc, w_ref[off:off + W_C, :])
    acc = acc + _bdot(yd_ref[...], w_ref[off + W_C:, :])
    o_ref[...] = acc


def _outproj(h, ya, yb, yc, yd, w_bf, tm, yc_transposed):
    t = h.shape[0]
    row = lambda n: pl.BlockSpec((tm, n), lambda i: (i, 0))
    ycs = pl.BlockSpec((W_C, tm), lambda i: (0, i)) if yc_transposed else row(W_C)
    return pl.pallas_call(
        functools.partial(_outproj_body, yc_transposed=yc_transposed),
        grid=(t // tm,),
        in_specs=[row(D_MODEL), row(W_A), row(W_B), ycs, row(W_D), _full((D_MODEL, D_MODEL))],
        out_specs=row(D_MODEL),
        out_shape=jax.ShapeDtypeStruct((t, D_MODEL), F32),
        compiler_params=_cparams("parallel"),
        name="outproj",
    )(h, ya, yb, yc, yd, w_bf)


def _cast_body(x_ref, o_ref):
    o_ref[...] = x_ref[...].astype(BF16)


def _to_bf16(x3, tr):
    e, r, c = x3.shape
    spec = pl.BlockSpec((1, tr, c), lambda i, j: (i, j, 0))
    return pl.pallas_call(
        _cast_body,
        grid=(e, r // tr),
        in_specs=[spec],
        out_specs=spec,
        out_shape=jax.ShapeDtypeStruct(x3.shape, BF16),
        compiler_params=_cparams("parallel", "parallel"),
        name="cast_bf16",
    )(x3)


def _ffn_body(blk_ref, h_ref, g_ref, w1_ref, w3_ref, w2_ref, o_ref, hn_s, acc_s, *, residual):
    del blk_ref
    j = pl.program_id(1)

    @pl.when(j == 0)
    def _():
        x = h_ref[...]
        hn_s[...] = (x * lax.rsqrt(jnp.mean(x * x, axis=-1, keepdims=True) + NORM_EPS) * g_ref[...]).astype(BF16)
        acc_s[...] = x if residual else jnp.zeros_like(x)

    hn = hn_s[...]
    a = jnp.dot(hn, w1_ref[0], preferred_element_type=F32)
    b = jnp.dot(hn, w3_ref[0], preferred_element_type=F32)
    acc_s[...] += _bdot(jax.nn.silu(a) * b, w2_ref[0])

    @pl.when(j == pl.num_programs(1) - 1)
    def _():
        o_ref[...] = acc_s[...]


def _ffn(blk_e, x, g, w1, w3, w2, tm, tf, residual):
    t = x.shape[0]
    ff = w1.shape[2]
    grid_spec = pltpu.PrefetchScalarGridSpec(
        num_scalar_prefetch=1,
        grid=(t // tm, ff // tf),
        in_specs=[pl.BlockSpec((tm, D_MODEL), lambda i, j, e: (i, 0)),
                  pl.BlockSpec((1, D_MODEL), lambda i, j, e: (0, 0)),
                  pl.BlockSpec((1, D_MODEL, tf), lambda i, j, e: (e[i], 0, j)),
                  pl.BlockSpec((1, D_MODEL, tf), lambda i, j, e: (e[i], 0, j)),
                  pl.BlockSpec((1, tf, D_MODEL), lambda i, j, e: (e[i], j, 0))],
        out_specs=pl.BlockSpec((tm, D_MODEL), lambda i, j, e: (i, 0)),
        scratch_shapes=[pltpu.VMEM((tm, D_MODEL), BF16), pltpu.VMEM((tm, D_MODEL), F32)],
    )
    return pl.pallas_call(
        functools.partial(_ffn_body, residual=residual),
        grid_spec=grid_spec,
        out_shape=jax.ShapeDtypeStruct((t, D_MODEL), F32),
        compiler_params=_cparams("parallel", "arbitrary"),
        name="ffn" if residual else "moe_ffn",
    )(blk_e, x, g.reshape(1, D_MODEL), w1, w3, w2)


def _router_body(h_ref, g_ref, w_ref, lg_ref):
    x = h_ref[...]
    hn = x * lax.rsqrt(jnp.mean(x * x, axis=-1, keepdims=True) + NORM_EPS) * g_ref[...]
    lg_ref[...] = _hdot(hn, w_ref[...])


def _router(h, g, w_pad, tm):
    t = h.shape[0]
    row = lambda n: pl.BlockSpec((tm, n), lambda i: (i, 0))
    return pl.pallas_call(
        _router_body,
        grid=(t // tm,),
        in_specs=[row(D_MODEL), _full((1, D_MODEL)), _full((D_MODEL, 128))],
        out_specs=row(128),
        out_shape=jax.ShapeDtypeStruct((t, 128), F32),
        compiler_params=_cparams("parallel"),
        name="router",
    )(h, g.reshape(1, D_MODEL), w_pad)


def _moe(h, g, router, w1, w3, w2, expert0, tm, tf):
    t = h.shape[0]
    logits = _router(h, g, jnp.pad(router, ((0, 0), (0, 128 - N_EXPERTS))), tm)
    top_logit, top_e = lax.top_k(logits[:, :N_EXPERTS], TOP_K)
    gates = jax.nn.softmax(top_logit, axis=-1)
    tk = t * TOP_K
    flat_e = top_e.reshape(tk)
    onehot = (flat_e[:, None] == jnp.arange(N_EXPERTS)[None, :]).astype(F32)
    seg = _row_tile(tk, RANK_SEG)
    oh3 = onehot.reshape(tk // seg, seg, N_EXPERTS)
    tril = jnp.tril(jnp.ones((seg, seg), F32))
    within = jnp.einsum('ts,bse->bte', tril, oh3, precision=lax.Precision.HIGHEST)
    seg_tot = within[:, -1, :]
    before = (within + (jnp.cumsum(seg_tot, axis=0) - seg_tot)[:, None, :]).reshape(tk, N_EXPERTS) - onehot
    rank = jnp.sum(onehot * before, axis=1).astype(jnp.int32)
    nblk = (jnp.sum(seg_tot, axis=0).astype(jnp.int32) + tm - 1) // tm
    blk_end = jnp.cumsum(nblk)
    dest = (blk_end - nblk)[flat_e] * tm + rank
    nb = -(-tk // tm) + N_EXPERTS
    row_tok = jnp.full((nb * tm,), t, jnp.int32).at[dest].set(jnp.arange(tk, dtype=jnp.int32) // TOP_K)
    blk_e = jnp.minimum(jnp.searchsorted(blk_end, jnp.arange(nb), side='right'), N_EXPERTS - 1).astype(jnp.int32)
    xg = jnp.concatenate([h, jnp.zeros((1, D_MODEL), F32)], axis=0)[row_tok]
    yb = _ffn(blk_e + expert0, xg, g, w1, w3, w2, tm, tf, False)
    d2 = dest.reshape(t, TOP_K)
    return h + (yb[d2[:, 0]] * gates[:, 0:1] + yb[d2[:, 1]] * gates[:, 1:2])


def _ple_body(h_ref, g_ref, wg_ref, p_ref, wp_ref, o_ref):
    x = h_ref[...]
    hn = (x * lax.rsqrt(jnp.mean(x * x, axis=-1, keepdims=True) + NORM_EPS) * g_ref[...]).astype(BF16)
    gate = jax.nn.sigmoid(jnp.dot(hn, wg_ref[...], preferred_element_type=F32))
    o_ref[...] = x + gate * _bdot(p_ref[0], wp_ref[...])


def _ple(h, g, wg_bf, p_all, layer, wp_bf, tm):
    t = h.shape[0]
    row = lambda n: pl.BlockSpec((tm, n), lambda i: (i, 0))
    return pl.pallas_call(
        _ple_body,
        grid=(t // tm,),
        in_specs=[row(D_MODEL), _full((1, D_MODEL)), _full((D_MODEL, D_MODEL)),
                  pl.BlockSpec((1, tm, PLE_DIM), lambda i: (layer, i, 0)), _full((PLE_DIM, D_MODEL))],
        out_specs=row(D_MODEL),
        out_shape=jax.ShapeDtypeStruct((t, D_MODEL), F32),
        compiler_params=_cparams("parallel"),
        name="ple",
    )(h, g.reshape(1, D_MODEL), wg_bf, p_all, wp_bf)


def _row_tile(t, want):
    tm = min(t, want)
    assert t % tm == 0, (t, tm)
    return tm


def _run(x, p, decode, wts, cache, state_pool, state_shift, state_wkv, page_table):
    nb, L, _ = x.shape
    t = nb * L
    tm = _row_tile(t, 512)
    n_past = page_table.shape[1] * cache[0].shape[3] if decode else 0
    h = x.reshape(t, D_MODEL)
    ks, vs, pools, shifts, wkvs, chunk_vs = [], [], [], [], [], []
    causal = jnp.tril(jnp.ones((CHUNK, CHUNK), bool))
    for l in range(DEPTH):
        w = wts[l]
        za, zb, qn, kn, v, zd, qh, kh, vh = _inproj(h, w["norm_mix_g"], w["w_in"], w["c_qnorm_g"], w["c_knorm_g"], tm)

        ws = jnp.where(causal, w["a_ws"], 0.0)
        bs = w["a_bs"]
        if decode:
            reps = CHUNK // L
            ws = jax.vmap(lambda m: jnp.kron(jnp.eye(reps, dtype=F32), m[:L, :L]))(ws)
            bs = jnp.tile(bs[:, :L], (1, reps))
        wcat = jnp.concatenate([ws[hh] for hh in range(H_A)], axis=1).astype(BF16)
        bias = jnp.repeat(bs.T, HEAD_DIM, axis=1)
        ya, va = _mix_a(za, w["a_vnorm_g"], wcat, bias, _row_tile(t, 512) // CHUNK)

        if decode:
            buf16 = jnp.pad(state_pool[l], ((0, 0), (POOL_HALO - POOL_BUF, 0), (0, 0)))
        else:
            buf16 = jnp.zeros((nb, POOL_HALO, W_B), F32)
        proj_bd = jax.scipy.linalg.block_diag(*[w["b_proj"][gi] for gi in range(len(POOL_WINDOWS))]).astype(BF16)
        tl_b = _row_tile(L, 512)
        yb, pool16 = _mix_b(zb.reshape(nb, L, W_B), buf16, proj_bd, w["b_scale"], _row_tile(nb, 512 // tl_b), tl_b,
                            n_past)
        yb = yb.reshape(t, W_B)

        lam_init = 0.8 - 0.6 * math.exp(-0.3 * l)
        lp = w["c_lambda"]
        lam = (jnp.exp(jnp.sum(lp[0] * lp[1])) - jnp.exp(jnp.sum(lp[2] * lp[3])) + lam_init).reshape(1)
        if decode:
            yc = _attn_decode(page_table, lam, qn, kn, v, w["c_subln_g"], cache[0], cache[1], l, L, 1.0 - lam_init)
        else:
            tq = _row_tile(L, ATTN_TQ)
            yc = _attn_prompt(lam, qh, kh, vh, w["c_subln_g"], nb, L, tq, _row_tile(L, max(tq, ATTN_TK)), ATTN_NSUB,
                              1.0 - lam_init)

        params = _rwkv_params(w["d_mu"], w["d_w0"], w["d_w2"], w["d_a0"], w["d_a2"], w["d_g2"], w["d_kk"], w["d_ka"],
                              w["d_rk"].reshape(-1), w["d_lnx_w"], w["d_lnx_b"])
        zd3 = zd.reshape(nb, L, N_COLS_D)
        if decode:
            yd, shift_new, wkv_new = _wkv_decode(zd3, state_shift, state_wkv, l, params, _row_tile(nb, 16))
        else:
            yd, shift_new, st = _wkv_prompt(zd3, jnp.zeros((nb, 1, N_COLS_D), F32), params, _row_tile(L, 512))
            st = st.reshape(nb, H_D, HEAD_DIM, H_D, HEAD_DIM)
            wkv_new = jnp.stack([st[:, hh, :, hh, :] for hh in range(H_D)], axis=1)
        h = _outproj(h, ya, yb, yc, yd.reshape(t, W_D), w["w_out"], tm, yc_transposed=not decode)

        if l % 2 == 0:
            h = _ffn(jnp.full((t // tm,), w["expert0"], jnp.int32), h, w["norm_ffn_g"], w["ffn_w1"], w["ffn_w3"],
                     w["ffn_w2"], tm, D_FF // 2, True)
        else:
            h = _moe(h, w["norm_ffn_g"], w["moe_router"], w["moe_w1"], w["moe_w3"], w["moe_w2"], w["expert0"], tm,
                     D_FF // 2)

        h = _ple(h, w["ple_norm_g"], w["ple_gate"], p.reshape(DEPTH, t, PLE_DIM), l, w["ple_proj"], tm)

        ks.append(kn.reshape(nb, L, H_C, 2, DK_C))
        vs.append(v.reshape(nb, L, H_C, HEAD_DIM))
        pools.append(pool16[:, POOL_HALO - POOL_BUF:])
        shifts.append(shift_new)
        wkvs.append(wkv_new)
        chunk_vs.append(va.reshape(nb, L, W_A))
        yield None
    yield (h.reshape(nb, L, D_MODEL), jnp.stack(ks), jnp.stack(vs), jnp.stack(pools), jnp.stack(shifts),
           jnp.stack(wkvs), jnp.stack(chunk_vs))


def kernel(x_prompt, x_sample, cache_k, cache_v, state_pool, state_shift, state_wkv, page_table, p_prompt, p_sample, norm_mix_g, w_in, w_out, a_vnorm_g, a_ws, a_bs, b_proj, b_scale, c_qnorm_g, c_knorm_g, c_lambda, c_subln_g, d_mu, d_w0, d_w2, d_a0, d_a2, d_g2, d_kk, d_ka, d_rk, d_lnx_w, d_lnx_b, norm_ffn_g, ffn_w1, ffn_w3, ffn_w2, moe_router, moe_w1, moe_w3, moe_w2, ple_norm_g, ple_gate, ple_proj):
    bf = lambda a: a.astype(BF16)
    ffn_bf = (bf(ffn_w1), bf(ffn_w3), bf(ffn_w2))
    moe_bf = (_to_bf16(moe_w1.reshape(-1, D_MODEL, D_FF), 512), _to_bf16(moe_w3.reshape(-1, D_MODEL, D_FF), 512),
              _to_bf16(moe_w2.reshape(-1, D_FF, D_MODEL), D_FF // 4))
    x_prompt, moe_bf = lax.optimization_barrier((x_prompt, moe_bf))
    wts = []
    for l in range(DEPTH):
        w = dict(norm_mix_g=norm_mix_g[l], w_in=bf(w_in[l]), w_out=bf(w_out[l]), a_vnorm_g=a_vnorm_g[l],
                 a_ws=a_ws[l], a_bs=a_bs[l], b_proj=b_proj[l], b_scale=b_scale[l], c_qnorm_g=c_qnorm_g[l],
                 c_knorm_g=c_knorm_g[l], c_lambda=c_lambda[l], c_subln_g=c_subln_g[l], d_mu=d_mu[l], d_w0=d_w0[l],
                 d_w2=d_w2[l], d_a0=d_a0[l], d_a2=d_a2[l], d_g2=d_g2[l], d_kk=d_kk[l], d_ka=d_ka[l], d_rk=d_rk[l],
                 d_lnx_w=d_lnx_w[l], d_lnx_b=d_lnx_b[l], norm_ffn_g=norm_ffn_g[l], ple_norm_g=ple_norm_g[l],
                 ple_gate=bf(ple_gate[l]), ple_proj=bf(ple_proj[l]))
        if l % 2 == 0:
            w.update(ffn_w1=ffn_bf[0], ffn_w3=ffn_bf[1], ffn_w2=ffn_bf[2], expert0=l // 2)
        else:
            w.update(moe_router=moe_router[l // 2], moe_w1=moe_bf[0], moe_w3=moe_bf[1], moe_w2=moe_bf[2],
                     expert0=N_EXPERTS * (l // 2))
        wts.append(w)
    depth, n_phys, page = cache_k.shape[:3]
    cache = (jnp.transpose(cache_k, (0, 1, 3, 4, 5, 2)).reshape(depth, n_phys, Q_C, page),
             jnp.transpose(cache_v, (0, 1, 3, 4, 2)).reshape(depth, n_phys, W_C, page))
    prompt = _run(x_prompt, p_prompt, False, wts, cache, state_pool, state_shift, state_wkv, page_table)
    sample = _run(x_sample, p_sample, True, wts, cache, state_pool, state_shift, state_wkv, page_table)
    for _ in range(DEPTH):
        next(prompt)
        next(sample)
    y_p, k_p, v_p, pool_p, shift_p, wkv_p, _ = next(prompt)
    y_s, k_s, v_s, pool_s, shift_s, wkv_s, cv_s = next(sample)
    return (y_p, y_s, k_p, v_p, pool_p, shift_p, wkv_p, k_s, v_s, pool_s, shift_s, wkv_s, cv_s)
```
